```python
import math
import jax, jax.numpy as jnp
from jax import lax
import numpy as np

D_MODEL = 1024
BATCH = 32
SEQ = 256
DEPTH = 2
DEC_BATCH = 4
DEC_SEQ = 4096
PAST_LEN = 256

GRID_W = 64
DN_HEADS = 4
DN_DK = 128
DN_DV = 128
DN_CONV_W = 3
DN_CHUNK = 64
CF_WIDTH = 512
CF_CONV_W = 31
SC_WIDTH = 512
SC_CONV_W = 3
D_FF = 2816
FFN_CONV_W = 3
N_BRANCH = 3
EPS = 1e-6

DN_QK = DN_HEADS * DN_DK
DN_V = DN_HEADS * DN_DV
SPLITS = (DN_QK, DN_QK, DN_V, DN_V, 2 * DN_HEADS, 2 * DN_HEADS, 2 * CF_WIDTH, 3 * SC_WIDTH, N_BRANCH * D_MODEL)
IN_COLS = sum(SPLITS)

kernel_name = "hybrid_bidir_deltanet_conformer_shortconv_diffusion_step"


def split_cols(x, sizes):
    idx = []
    acc = 0
    for s in sizes[:-1]:
        acc += s
        idx.append(acc)
    return jnp.split(x, idx, axis=-1)


def rmsnorm(x, g):
    x32 = x.astype(jnp.float32)
    y = x32 * lax.rsqrt(jnp.mean(x32 * x32, axis=-1, keepdims=True) + EPS)
    return (y * g.astype(jnp.float32)).astype(x.dtype)


def layernorm(x, g, b):
    x32 = x.astype(jnp.float32)
    mu = jnp.mean(x32, axis=-1, keepdims=True)
    xc = x32 - mu
    y = xc * lax.rsqrt(jnp.mean(xc * xc, axis=-1, keepdims=True) + EPS)
    return (y * g.astype(jnp.float32) + b.astype(jnp.float32)).astype(x.dtype)


def l2norm(x):
    return x * lax.rsqrt(jnp.sum(x * x, axis=-1, keepdims=True) + EPS)


def dwconv_seq(x, w):
    K, C = w.shape
    return lax.conv_general_dilated(x, w[:, None, :].astype(x.dtype), (1,), [(K // 2, K // 2)],
                                    dimension_numbers=("NWC", "WIO", "NWC"), feature_group_count=C)


def dwconv(x, w, axis, on_grid):
    if not on_grid:
        return dwconv_seq(x, w)
    B, L, C = x.shape
    rows = L // GRID_W
    if axis == "h":
        return dwconv_seq(x.reshape(B * rows, GRID_W, C), w).reshape(B, L, C)
    xg = x.reshape(B, rows, GRID_W, C).transpose(0, 2, 1, 3).reshape(B * GRID_W, rows, C)
    y = dwconv_seq(xg, w)
    return y.reshape(B, GRID_W, rows, C).transpose(0, 2, 1, 3).reshape(B, L, C)


def chunk_gated_delta(q, k, v, g, beta, s0):
    B, L, H, DK = q.shape
    DV = v.shape[-1]
    C = DN_CHUNK
    N = L // C

    def blk(t):
        t = t.reshape((B, N, C, H) + t.shape[3:])
        return jnp.moveaxis(t, 3, 1)

    q, k, v, g, beta = blk(q), blk(k), blk(v), blk(g), blk(beta)
    gam = jnp.cumsum(g, axis=3)
    incl = jnp.tril(jnp.ones((C, C), dtype=bool))
    strict = jnp.tril(jnp.ones((C, C), dtype=bool), -1)
    diff = gam[..., :, None] - gam[..., None, :]
    decay = jnp.where(incl, jnp.exp(jnp.where(incl, diff, 0.0)), 0.0)
    a = jnp.where(strict, beta[..., :, None] * jnp.einsum('bhnid,bhnjd->bhnij', k, k) * decay, 0.0)
    rhs = jnp.concatenate([beta[..., None] * v, (beta * jnp.exp(gam))[..., None] * k], axis=-1)
    sol = lax.linalg.triangular_solve(a, rhs, left_side=True, lower=True, unit_diagonal=True)
    u, w = sol[..., :DV], sol[..., DV:]
    qk = jnp.einsum('bhnid,bhnjd->bhnij', q, k) * decay
    q_dec = q * jnp.exp(gam)[..., None]
    k_dec = k * jnp.exp(gam[..., -1:] - gam)[..., None]
    g_tot = jnp.exp(gam[..., -1])

    def step(s, xs):
        u_n, w_n, qk_n, qd_n, kd_n, gt_n = xs
        v_new = u_n - jnp.einsum('bhck,bhkv->bhcv', w_n, s)
        o = jnp.einsum('bhck,bhkv->bhcv', qd_n, s) + jnp.einsum('bhij,bhjv->bhiv', qk_n, v_new)
        s = s * gt_n[..., None, None] + jnp.einsum('bhck,bhcv->bhkv', kd_n, v_new)
        return s, o

    xs = tuple(jnp.moveaxis(t, 2, 0) for t in (u, w, qk, q_dec, k_dec, g_tot))
    s_fin, o = lax.scan(step, s0, xs)
    o = jnp.transpose(o, (1, 0, 3, 2, 4)).reshape(B, L, H, DV)
    return o, s_fin


def trunk_layer(x, mod, p, s0f, s0b, on_grid):
    B, L, D = x.shape
    f32 = jnp.float32
    sh1, sc1, gt1, sh2, sc2, gt2 = jnp.split(mod, 6, axis=-1)

    h = rmsnorm(x, p["g_pre_mix"]) * (1 + sc1) + sh1
    proj = h @ p["w_in"]
    q, k, v, z, a_dir, b_dir, cf_in, sc_in, gate_pre = split_cols(proj, SPLITS)

    qkv = jax.nn.silu(dwconv(jnp.concatenate([q, k, v], axis=-1), p["dn_conv"], "h", on_grid))
    q, k, v = split_cols(qkv, (DN_QK, DN_QK, DN_V))
    q = l2norm(q.reshape(B, L, DN_HEADS, DN_DK).astype(f32)) * (DN_DK ** -0.5)
    k = l2norm(k.reshape(B, L, DN_HEADS, DN_DK).astype(f32))
    v = v.reshape(B, L, DN_HEADS, DN_DV).astype(f32)
    a_dir = a_dir.astype(f32).reshape(B, L, 2, DN_HEADS)
    beta = jax.nn.sigmoid(b_dir.astype(f32).reshape(B, L, 2, DN_HEADS))
    g = -jnp.exp(p["dn_a_log"].astype(f32)) * jax.nn.softplus(a_dir + p["dn_dt_bias"].astype(f32))
    o_f, s_f = chunk_gated_delta(q, k, v, g[:, :, 0], beta[:, :, 0], s0f)
    o_b, s_b = chunk_gated_delta(jnp.flip(q, 1), jnp.flip(k, 1), jnp.flip(v, 1),
                                 jnp.flip(g[:, :, 1], 1), jnp.flip(beta[:, :, 1], 1), s0b)
    o = rmsnorm(o_f + jnp.flip(o_b, 1), p["dn_norm_g"]).astype(x.dtype)
    o = o * jax.nn.silu(z.reshape(B, L, DN_HEADS, DN_DV))
    y_dn = o.reshape(B, L, DN_V) @ p["w_dn_out"]

    ga, gb = jnp.split(cf_in, 2, axis=-1)
    hc = dwconv(ga * jax.nn.sigmoid(gb), p["cf_conv"], "h", on_grid)
    hc = jax.nn.silu(layernorm(hc, p["cf_ln_g"], p["cf_ln_b"]))
    y_cf = hc @ p["w_cf_out"]

    bg, cg, xh = jnp.split(sc_in, 3, axis=-1)
    y_sc = (bg * dwconv(cg * xh, p["sc_conv"], "v", on_grid)) @ p["w_sc_out"]

    g_a, g_b, g_c = jnp.split(jax.nn.sigmoid(gate_pre), 3, axis=-1)
    m = (g_a * y_dn + g_b * y_cf + g_c * y_sc) @ p["w_o"]
    x = x + gt1 * rmsnorm(m, p["g_post_mix"])

    h = rmsnorm(x, p["g_pre_ffn"]) * (1 + sc2) + sh2
    u = dwconv(h @ p["w_ffn_up"], p["ffn_conv"], "v", on_grid)
    ua, ub = jnp.split(u, 2, axis=-1)
    y = (jax.nn.silu(ua) * ub) @ p["w_ffn_down"]
    x = x + gt2 * rmsnorm(y, p["g_post_ffn"])
    return x, s_f, s_b


def setup_inputs(seed: int = 0) -> dict:
    key = jax.random.key(seed)
    ks = jax.random.split(key, 32)
    f32 = jnp.float32
    D = D_MODEL

    def nrm(k, shape, scale):
        return jax.random.normal(k, shape, f32) * scale

    def gain(k, shape):
        return 1.0 + 0.05 * jax.random.normal(k, shape, f32)

    dt = jnp.exp(jax.random.uniform(ks[13], (DEPTH, 2, DN_HEADS), f32, math.log(1e-3), math.log(1e-1)))
    dt_bias = dt + jnp.log(-jnp.expm1(-dt))
    a_log = jnp.log(jax.random.uniform(ks[12], (DEPTH, 2, DN_HEADS), f32, 1.0, 16.0))
    return {
        "x_prompt": nrm(ks[0], (BATCH, SEQ, D), 1.0),
        "x_sample": nrm(ks[1], (DEC_BATCH, DEC_SEQ, D), 1.0),
        "state_dn": nrm(ks[2], (DEC_BATCH, DEPTH, 2, DN_HEADS, DN_DK, DN_DV), DN_DK ** -0.5),
        "c": nrm(ks[3], (DEC_BATCH, D), 1.0),
        "c_ctx": nrm(ks[4], (D,), 1.0),
        "w_mod": nrm(ks[5], (DEPTH, D, 6 * D), 0.5 * D ** -0.5),
        "b_mod": nrm(ks[6], (DEPTH, 6 * D), 0.02),
        "g_pre_mix": gain(ks[7], (DEPTH, D)),
        "g_post_mix": gain(ks[8], (DEPTH, D)),
        "g_pre_ffn": gain(ks[9], (DEPTH, D)),
        "g_post_ffn": gain(ks[10], (DEPTH, D)),
        "w_in": nrm(ks[11], (DEPTH, D, IN_COLS), D ** -0.5),
        "dn_conv": nrm(ks[14], (DEPTH, DN_CONV_W, 2 * DN_QK + DN_V), DN_CONV_W ** -0.5),
        "dn_a_log": a_log,
        "dn_dt_bias": dt_bias,
        "dn_norm_g": gain(ks[15], (DEPTH, DN_DV)),
        "w_dn_out": nrm(ks[16], (DEPTH, DN_V, D), DN_V ** -0.5),
        "cf_conv": nrm(ks[17], (DEPTH, CF_CONV_W, CF_WIDTH), CF_CONV_W ** -0.5),
        "cf_ln_g": gain(ks[18], (DEPTH, CF_WIDTH)),
        "cf_ln_b": nrm(ks[19], (DEPTH, CF_WIDTH), 0.02),
        "w_cf_out": nrm(ks[20], (DEPTH, CF_WIDTH, D), CF_WIDTH ** -0.5),
        "sc_conv": nrm(ks[21], (DEPTH, SC_CONV_W, SC_WIDTH), SC_CONV_W ** -0.5),
        "w_sc_out": nrm(ks[22], (DEPTH, SC_WIDTH, D), SC_WIDTH ** -0.5),
        "w_o": nrm(ks[23], (DEPTH, D, D), D ** -0.5),
        "w_ffn_up": nrm(ks[24], (DEPTH, D, 2 * D_FF), D ** -0.5),
        "ffn_conv": nrm(ks[25], (DEPTH, FFN_CONV_W, 2 * D_FF), FFN_CONV_W ** -0.5),
        "w_ffn_down": nrm(ks[26], (DEPTH, D_FF, D), D_FF ** -0.5),
    }


def reference(x_prompt, x_sample, state_dn, c, c_ctx, w_mod, b_mod, g_pre_mix, g_post_mix, g_pre_ffn, g_post_ffn,
              w_in, dn_conv, dn_a_log, dn_dt_bias, dn_norm_g, w_dn_out, cf_conv, cf_ln_g, cf_ln_b, w_cf_out,
              sc_conv, w_sc_out, w_o, w_ffn_up, ffn_conv, w_ffn_down):
    xp = x_prompt
    xs = x_sample
    zeros = jnp.zeros((x_prompt.shape[0], DN_HEADS, DN_DK, DN_DV), jnp.float32)
    ctx_states = []
    for l in range(DEPTH):
        p = {
            "g_pre_mix": g_pre_mix[l], "g_post_mix": g_post_mix[l],
            "g_pre_ffn": g_pre_ffn[l], "g_post_ffn": g_post_ffn[l],
            "w_in": w_in[l], "dn_conv": dn_conv[l], "dn_a_log": dn_a_log[l], "dn_dt_bias": dn_dt_bias[l],
            "dn_norm_g": dn_norm_g[l], "w_dn_out": w_dn_out[l], "cf_conv": cf_conv[l],
            "cf_ln_g": cf_ln_g[l], "cf_ln_b": cf_ln_b[l], "w_cf_out": w_cf_out[l],
            "sc_conv": sc_conv[l], "w_sc_out": w_sc_out[l], "w_o": w_o[l],
            "w_ffn_up": w_ffn_up[l], "ffn_conv": ffn_conv[l], "w_ffn_down": w_ffn_down[l],
        }
        mod_ctx = (jax.nn.silu(c_ctx) @ w_mod[l] + b_mod[l])[None, None, :]
        mod_lat = (jax.nn.silu(c) @ w_mod[l] + b_mod[l])[:, None, :]
        xp, s_f, s_b = trunk_layer(xp, mod_ctx, p, zeros, zeros, False)
        ctx_states.append(jnp.stack([s_f, s_b], axis=1))
        st = state_dn[:, l].astype(jnp.float32)
        xs, _, _ = trunk_layer(xs, mod_lat, p, st[:, 0], st[:, 1], True)
    new_state_dn = jnp.stack(ctx_states, axis=1).astype(x_prompt.dtype)
    return (xp, xs, new_state_dn)
```

```python
import functools

import numpy as np
import jax
import jax.numpy as jnp
from jax import lax
from jax.experimental import pallas as pl
from jax.experimental.pallas import tpu as pltpu

F32 = jnp.float32
BF16 = jnp.bfloat16

D_MODEL = 1024
DEPTH = 2
GRID_W = 64
DN_HEADS = 4
DN_DK = 128
DN_DV = 128
DN_CHUNK = 64
DN_QK = DN_HEADS * DN_DK
DN_V = DN_HEADS * DN_DV
CF_WIDTH = 512
CF_CONV_W = 31
SC_WIDTH = 512
D_FF = 2816
EPS = 1e-6

_OFF_Q = 0
_OFF_Z = 2 * DN_QK + DN_V
_OFF_A = _OFF_Z + DN_V
_OFF_CF = _OFF_A + 4 * DN_HEADS
_OFF_SC = _OFF_CF + 2 * CF_WIDTH
_OFF_GATE = _OFF_SC + 3 * SC_WIDTH

TOKENS_PER_TILE = 512
COL_TILE_W = 8
CONV_GAP = 16
ROW_BLOCK = 64
FF_BLOCK = 256
PACK_W = 128
VMEM_LIMIT_BYTES = 56 * 1024 * 1024


def _sigmoid(x):
    return 0.5 * jnp.tanh(0.5 * x) + 0.5


def _silu(x):
    return x * _sigmoid(x)


def _softplus(x):
    return jnp.maximum(x, 0.0) + jnp.log1p(jnp.exp(-jnp.abs(x)))


def _mm(a, b):
    return jnp.dot(a.astype(BF16), b.astype(BF16), preferred_element_type=F32)


def _mm_nt(a, b):
    return lax.dot_general(a.astype(BF16), b.astype(BF16), (((1,), (1,)), ((), ())),
                           preferred_element_type=F32)


def _modulated_rmsnorm(x, gain, shift, scale):
    y = x * lax.rsqrt(jnp.mean(x * x, axis=-1, keepdims=True) + EPS)
    return (y * gain) * (1.0 + scale) + shift


def _rmsnorm(x, gain):
    return x * lax.rsqrt(jnp.mean(x * x, axis=-1, keepdims=True) + EPS) * gain


def _store_segments(pad_ref, val, seg, gap):
    rows, width = val.shape
    zeros = jnp.zeros((gap, width), F32)
    pad_ref[pl.ds(0, gap), pl.ds(0, width)] = zeros
    for s in range(rows // seg):
        base = gap + s * (seg + gap)
        pad_ref[pl.ds(base, seg), pl.ds(0, width)] = val[s * seg:(s + 1) * seg]
        pad_ref[pl.ds(base + seg, gap), pl.ds(0, width)] = zeros


def _conv_block(pad_ref, w_ref, seg, gap, stride, s, r0, rows, c0, cols, w_c0):
    taps = w_ref.shape[0]
    base = gap + s * (seg + gap) + r0
    acc = None
    for k in range(taps):
        off = base + (k - taps // 2) * stride
        term = pad_ref[pl.ds(off, rows), pl.ds(c0, cols)] * w_ref[pl.ds(k, 1), pl.ds(w_c0, cols)]
        acc = term if acc is None else acc + term
    return acc


def _mod_kernel(c_ref, w_ref, b_ref, o_ref):
    s = _silu(c_ref[...])
    o_ref[0] = _mm(s, w_ref[0]) + b_ref[0]


def _modulation(c_all, w_mod, b_mod):
    tn = 1024
    n_cols = 6 * D_MODEL
    return pl.pallas_call(
        _mod_kernel,
        out_shape=jax.ShapeDtypeStruct((DEPTH, 8, n_cols), F32),
        grid=(DEPTH, n_cols // tn),
        in_specs=[
            pl.BlockSpec((8, D_MODEL), lambda l, j: (0, 0)),
            pl.BlockSpec((1, D_MODEL, tn), lambda l, j: (l, 0, j)),
            pl.BlockSpec((1, 1, tn), lambda l, j: (l, 0, j)),
        ],
        out_specs=pl.BlockSpec((1, 8, tn), lambda l, j: (l, 0, j)),
        compiler_params=pltpu.CompilerParams(dimension_semantics=("parallel", "parallel"),
                                             vmem_limit_bytes=VMEM_LIMIT_BYTES),
        name="modulation",
    )(c_all, w_mod, b_mod.reshape(DEPTH, 1, n_cols))


def _mixer_in_kernel(x_ref, mod_ref, gpre_ref, wqkv_ref, wgate_ref, wcf_ref, wsc_ref,
                     dnconv_ref, cfconv_ref, lng_ref, lnb_ref, alog_ref, dtb_ref,
                     trif_ref, trib_ref, ones_ref,
                     qkv_out, pack_out, hc_out, cgxh_out,
                     pad_ref, *, seg):
    tm = x_ref.shape[0]
    nseg = tm // seg
    gap = CONV_GAP
    hb = _modulated_rmsnorm(x_ref[...], gpre_ref[...], mod_ref[0, 0:1, :], mod_ref[0, 1:2, :]).astype(BF16)

    for grp in range(3):
        proj = jnp.dot(hb, wqkv_ref[:, grp * DN_QK:(grp + 1) * DN_QK], preferred_element_type=F32)
        _store_segments(pad_ref, proj, seg, gap)
        for s in range(nseg):
            for r0 in range(0, seg, ROW_BLOCK):
                act = _silu(_conv_block(pad_ref, dnconv_ref, seg, gap, 1, s, r0, ROW_BLOCK, 0, DN_QK,
                                        grp * DN_QK))
                rows = pl.ds(s * seg + r0, ROW_BLOCK)
                if grp == 2:
                    qkv_out[rows, pl.ds(2 * DN_QK, DN_V)] = act
                else:
                    post = DN_DK ** -0.5 if grp == 0 else 1.0
                    for hd in range(DN_HEADS):
                        a = act[:, hd * DN_DK:(hd + 1) * DN_DK]
                        a = a * lax.rsqrt(jnp.sum(a * a, axis=-1, keepdims=True) + EPS)
                        if grp == 0:
                            a = a * post
                        qkv_out[rows, pl.ds(grp * DN_QK + hd * DN_DK, DN_DK)] = a

    nh2 = 2 * DN_HEADS
    ab = jnp.dot(hb, wgate_ref[...], preferred_element_type=F32)
    g = -jnp.exp(alog_ref[...]) * _softplus(ab[:, 0:nh2] + dtb_ref[...])
    beta = _sigmoid(ab[:, nh2:2 * nh2])
    g1 = g.astype(BF16)
    r1 = g - g1.astype(F32)
    g2 = r1.astype(BF16)
    g3 = (r1 - g2.astype(F32)).astype(BF16)
    pieces = jnp.concatenate([g1, g2, g3], axis=1)

    def summed(mat_ref):
        c = jnp.dot(mat_ref[...], pieces, preferred_element_type=F32)
        return c[:, 0:nh2] + c[:, nh2:2 * nh2] + c[:, 2 * nh2:3 * nh2]

    lane = lax.broadcasted_iota(jnp.int32, (tm, nh2), 1)
    gam = jnp.where(lane < DN_HEADS, summed(trif_ref), summed(trib_ref))
    tot = summed(ones_ref)
    expg = jnp.exp(gam)
    pack_out[...] = jnp.concatenate(
        [gam, beta, expg, beta * expg, jnp.exp(tot - gam), jnp.exp(tot),
         jnp.zeros((tm, PACK_W - 6 * nh2), F32)], axis=1)

    pc = jnp.dot(hb, wcf_ref[...], preferred_element_type=F32)
    glu = pc[:, :CF_WIDTH] * _sigmoid(pc[:, CF_WIDTH:])
    _store_segments(pad_ref, glu, seg, gap)
    for s in range(nseg):
        for r0 in range(0, seg, ROW_BLOCK):
            c = _conv_block(pad_ref, cfconv_ref, seg, gap, 1, s, r0, ROW_BLOCK, 0, CF_WIDTH, 0)
            mu = jnp.mean(c, axis=-1, keepdims=True)
            cc = c - mu
            y = cc * lax.rsqrt(jnp.mean(cc * cc, axis=-1, keepdims=True) + EPS)
            y = y * lng_ref[...] + lnb_ref[...]
            hc_out[pl.ds(s * seg + r0, ROW_BLOCK), :] = _silu(y)

    ps = jnp.dot(hb, wsc_ref[...], preferred_element_type=F32)
    cgxh_out[...] = ps[:, :SC_WIDTH] * ps[:, SC_WIDTH:]


def _chunk_matrices(tm):
    idx = np.arange(tm)
    same = (idx[:, None] // DN_CHUNK) == (idx[None, :] // DN_CHUNK)
    lower = same & (idx[:, None] >= idx[None, :])
    upper = same & (idx[:, None] <= idx[None, :])
    as_bf16 = lambda m: jnp.asarray(m.astype(np.float32), dtype=BF16)
    return as_bf16(lower), as_bf16(upper), as_bf16(same)


def _const_spec(shape):
    nd = len(shape)
    return pl.BlockSpec(shape, lambda *_: (0,) * nd, pipeline_mode=pl.Buffered(1))


def _mixer_in(x2d, mod, p, *, seq_len, seg):
    tokens = x2d.shape[0]
    tm = TOKENS_PER_TILE
    tiles_per_mod = max(seq_len // tm, 1) if mod.shape[0] > 1 else None
    mod_map = (lambda i: (i // tiles_per_mod, 0, 0)) if tiles_per_mod else (lambda i: (0, 0, 0))
    trif, trib, ones = _chunk_matrices(tm)
    pad_rows = CONV_GAP + (tm // seg) * (seg + CONV_GAP)
    row = lambda i: (i, 0)
    consts = [p["g_pre_mix"], p["w_qkv"], p["w_dngate"], p["w_cf"], p["w_sc"], p["dn_conv"], p["cf_conv"],
              p["cf_ln_g"], p["cf_ln_b"], p["dn_a_log"], p["dn_dt_bias"], trif, trib, ones]
    return pl.pallas_call(
        functools.partial(_mixer_in_kernel, seg=seg),
        out_shape=(jax.ShapeDtypeStruct((tokens, 2 * DN_QK + DN_V), F32),
                   jax.ShapeDtypeStruct((tokens, PACK_W), F32),
                   jax.ShapeDtypeStruct((tokens, CF_WIDTH), F32),
                   jax.ShapeDtypeStruct((tokens, SC_WIDTH), F32)),
        grid=(tokens // tm,),
        in_specs=[pl.BlockSpec((tm, D_MODEL), row), pl.BlockSpec((1, 6, D_MODEL), mod_map)]
                 + [_const_spec(c.shape) for c in consts],
        out_specs=(pl.BlockSpec((tm, 2 * DN_QK + DN_V), row), pl.BlockSpec((tm, PACK_W), row),
                   pl.BlockSpec((tm, CF_WIDTH), row), pl.BlockSpec((tm, SC_WIDTH), row)),
        scratch_shapes=[pltpu.VMEM((pad_rows, DN_QK), F32)],
        compiler_params=pltpu.CompilerParams(dimension_semantics=("parallel",),
                                             vmem_limit_bytes=VMEM_LIMIT_BYTES),
        name="mixer_in",
    )(x2d, mod, *consts)


def _unit_triangular_inverse(a_strict, same_pair, level_masks, eye):
    inv = eye - jnp.where(same_pair, a_strict, 0.0)
    for mask in level_masks:
        off = jnp.where(mask, a_strict, 0.0)
        inv = inv - _mm(_mm(inv, off), inv)
    return inv


def _deltanet_kernel(qf_ref, qb_ref, pf_ref, pb_ref, *rest, chunks, has_s0):
    if has_s0:
        s0_ref, of_ref, ob_ref, sout_ref, state_ref = rest
    else:
        of_ref, ob_ref, sout_ref, state_ref = rest
    n = pl.program_id(1)
    last = pl.num_programs(1) - 1
    c = DN_CHUNK

    @pl.when(n == 0)
    def _():
        if has_s0:
            state_ref[...] = s0_ref[0]
        else:
            state_ref[...] = jnp.zeros(state_ref.shape, F32)

    ii = lax.broadcasted_iota(jnp.int32, (c, c), 0)
    jj = lax.broadcasted_iota(jnp.int32, (c, c), 1)
    eye = (ii == jj).astype(F32)
    dir_masks = []
    for rev in (False, True):
        incl = (ii <= jj) if rev else (ii >= jj)
        strict = (ii < jj) if rev else (ii > jj)
        same_pair = strict & ((ii // 2) == (jj // 2))
        levels = []
        size = 2
        while size < c:
            levels.append(strict & ((ii // (2 * size)) == (jj // (2 * size))) & ((ii // size) != (jj // size)))
            size *= 2
        dir_masks.append((incl, strict, same_pair, levels))

    nh2 = 2 * DN_HEADS

    def body(i, carry):
        for d, (q_ref, p_ref, o_ref) in enumerate(((qf_ref, pf_ref, of_ref), (qb_ref, pb_ref, ob_ref))):
            incl, strict, same_pair, levels = dir_masks[d]
            ci = i if d == 0 else chunks - 1 - i
            rows = pl.ds(pl.multiple_of(ci * c, c), c)
            pk = p_ref[rows, :]
            pk_t = pk.T
            for hd in range(DN_HEADS):
                j = d * DN_HEADS + hd
                col = lambda grp: pk[:, grp * nh2 + j:grp * nh2 + j + 1]
                gam_c, beta_c, expg_c, bexpg_c, expd_c = col(0), col(1), col(2), col(3), col(4)
                etot = pk[0:1, 5 * nh2 + j:5 * nh2 + j + 1]
                gam_r = pk_t[j:j + 1, :]
                q = q_ref[rows, pl.ds(hd * DN_DK, DN_DK)]
                k = q_ref[rows, pl.ds(DN_QK + hd * DN_DK, DN_DK)]
                v = q_ref[rows, pl.ds(2 * DN_QK + hd * DN_DV, DN_DV)]
                decay = jnp.where(incl, jnp.exp(jnp.where(incl, gam_c - gam_r, 0.0)), 0.0)
                a = jnp.where(strict, beta_c * _mm_nt(k, k) * decay, 0.0)
                inv = _unit_triangular_inverse(a, same_pair, levels, eye)
                rhs = jnp.concatenate([beta_c * v, bexpg_c * k], axis=1)
                sol = _mm(inv, rhs)
                u, w = sol[:, :DN_DV], sol[:, DN_DV:]
                qk = _mm_nt(q, k) * decay
                s = state_ref[d, hd]
                v_new = u - _mm(w, s)
                o_ref[rows, pl.ds(hd * DN_DV, DN_DV)] = _mm(q * expg_c, s) + _mm(qk, v_new)
                state_ref[d, hd] = s * etot + _mm((k * expd_c).T, v_new)
        return carry

    lax.fori_loop(0, chunks, body, 0)

    @pl.when(n == last)
    def _():
        sout_ref[0] = state_ref[...]


def _deltanet(qkv, pack, s0, *, batch, seq_len, chunks):
    tokens = qkv.shape[0]
    rows = chunks * DN_CHUNK
    nb = seq_len // rows
    fwd = lambda b, n: (b * nb + n, 0)
    bwd = lambda b, n: (b * nb + nb - 1 - n, 0)
    state_block = (1, 2, DN_HEADS, DN_DK, DN_DV)
    state_map = lambda b, n: (b, 0, 0, 0, 0)
    in_specs = [pl.BlockSpec((rows, 2 * DN_QK + DN_V), fwd), pl.BlockSpec((rows, 2 * DN_QK + DN_V), bwd),
                pl.BlockSpec((rows, PACK_W), fwd), pl.BlockSpec((rows, PACK_W), bwd)]
    args = [qkv, qkv, pack, pack]
    if s0 is not None:
        in_specs.append(pl.BlockSpec(state_block, state_map))
        args.append(s0)
    return pl.pallas_call(
        functools.partial(_deltanet_kernel, chunks=chunks, has_s0=s0 is not None),
        out_shape=(jax.ShapeDtypeStruct((tokens, DN_V), F32), jax.ShapeDtypeStruct((tokens, DN_V), F32),
                   jax.ShapeDtypeStruct((batch, 2, DN_HEADS, DN_DK, DN_DV), F32)),
        grid=(batch, nb),
        in_specs=in_specs,
        out_specs=(pl.BlockSpec((rows, DN_V), fwd), pl.BlockSpec((rows, DN_V), bwd),
                   pl.BlockSpec(state_block, state_map)),
        scratch_shapes=[pltpu.VMEM((2, DN_HEADS, DN_DK, DN_DV), F32)],
        compiler_params=pltpu.CompilerParams(dimension_semantics=("parallel", "arbitrary"),
                                             vmem_limit_bytes=VMEM_LIMIT_BYTES),
        name="deltanet",
    )(*args)


def _mixer_out_kernel(x_ref, of_ref, ob_ref, hc_ref, cg_ref, mod_ref, gpre_ref, gpost_ref, dnn_ref,
                      wz_ref, wbg_ref, wmg_ref, wdn_ref, wcf_ref, wsc_ref, wo_ref, scconv_ref,
                      out_ref, pad_ref, min_ref, *, seg, stride):
    tm = TOKENS_PER_TILE
    gap = CONV_GAP
    x = x_ref[...].reshape(tm, D_MODEL)
    hb = _modulated_rmsnorm(x, gpre_ref[...], mod_ref[0, 0:1, :], mod_ref[0, 1:2, :]).astype(BF16)

    o = of_ref[...].reshape(tm, DN_V) + ob_ref[...].reshape(tm, DN_V)
    z = jnp.dot(hb, wz_ref[...], preferred_element_type=F32)
    heads = []
    for hd in range(DN_HEADS):
        oh = _rmsnorm(o[:, hd * DN_DV:(hd + 1) * DN_DV], dnn_ref[...])
        heads.append((oh * _silu(z[:, hd * DN_DV:(hd + 1) * DN_DV])).astype(BF16))
    o_gated = jnp.concatenate(heads, axis=1)

    _store_segments(pad_ref, cg_ref[...].reshape(tm, SC_WIDTH), seg, gap)
    bg = jnp.dot(hb, wbg_ref[...], preferred_element_type=F32)
    sc_parts = []
    for s in range(tm // seg):
        for r0 in range(0, seg, ROW_BLOCK):
            conv = _conv_block(pad_ref, scconv_ref, seg, gap, stride, s, r0, ROW_BLOCK, 0, SC_WIDTH, 0)
            sc_parts.append(conv)
    sc_in = (bg * jnp.concatenate(sc_parts, axis=0)).astype(BF16)
    hc = hc_ref[...].reshape(tm, CF_WIDTH).astype(BF16)

    cb = 256
    for c0 in range(0, D_MODEL, cb):
        cols = slice(c0, c0 + cb)
        y_dn = jnp.dot(o_gated, wdn_ref[:, cols], preferred_element_type=F32)
        y_cf = jnp.dot(hc, wcf_ref[:, cols], preferred_element_type=F32)
        y_sc = jnp.dot(sc_in, wsc_ref[:, cols], preferred_element_type=F32)
        ga = _sigmoid(jnp.dot(hb, wmg_ref[:, c0:c0 + cb], preferred_element_type=F32))
        gb = _sigmoid(jnp.dot(hb, wmg_ref[:, D_MODEL + c0:D_MODEL + c0 + cb], preferred_element_type=F32))
        gc = _sigmoid(jnp.dot(hb, wmg_ref[:, 2 * D_MODEL + c0:2 * D_MODEL + c0 + cb], preferred_element_type=F32))
        min_ref[:, cols] = (ga * y_dn + gb * y_cf + gc * y_sc).astype(BF16)

    m = jnp.dot(min_ref[...], wo_ref[...], preferred_element_type=F32)
    out = x + mod_ref[0, 2:3, :] * _rmsnorm(m, gpost_ref[...])
    out_ref[...] = out.reshape(out_ref.shape)


def _ffn_kernel(x_ref, mod_ref, gpre_ref, gpost_ref, wup_ref, wdown_ref, conv_ref,
                out_ref, pad_a, pad_b, *, seg, stride):
    tm = TOKENS_PER_TILE
    gap = CONV_GAP
    x = x_ref[...].reshape(tm, D_MODEL)
    hb = _modulated_rmsnorm(x, gpre_ref[...], mod_ref[0, 3:4, :], mod_ref[0, 4:5, :]).astype(BF16)
    y = jnp.zeros((tm, D_MODEL), F32)
    for f0 in range(0, D_FF, FF_BLOCK):
        ua = jnp.dot(hb, wup_ref[:, f0:f0 + FF_BLOCK], preferred_element_type=F32)
        ub = jnp.dot(hb, wup_ref[:, D_FF + f0:D_FF + f0 + FF_BLOCK], preferred_element_type=F32)
        _store_segments(pad_a, ua, seg, gap)
        _store_segments(pad_b, ub, seg, gap)
        parts = []
        for s in range(tm // seg):
            for r0 in range(0, seg, ROW_BLOCK):
                ca = _conv_block(pad_a, conv_ref, seg, gap, stride, s, r0, ROW_BLOCK, 0, FF_BLOCK, f0)
                cb = _conv_block(pad_b, conv_ref, seg, gap, stride, s, r0, ROW_BLOCK, 0, FF_BLOCK, D_FF + f0)
                parts.append((_silu(ca) * cb).astype(BF16))
        act = jnp.concatenate(parts, axis=0)
        y = y + jnp.dot(act, wdown_ref[f0:f0 + FF_BLOCK, :], preferred_element_type=F32)
    out = x + mod_ref[0, 5:6, :] * _rmsnorm(y, gpost_ref[...])
    out_ref[...] = out.reshape(out_ref.shape)


def _tile_view(a2d, on_grid, batch):
    width = a2d.shape[1]
    if on_grid:
        return a2d.reshape(batch, a2d.shape[0] // (batch * GRID_W), GRID_W, width)
    return a2d.reshape(a2d.shape[0] // TOKENS_PER_TILE, TOKENS_PER_TILE // COL_TILE_W, COL_TILE_W, width)


def _tile_spec(width, on_grid):
    block = (1, TOKENS_PER_TILE // COL_TILE_W, COL_TILE_W, width)
    if on_grid:
        per_batch = GRID_W // COL_TILE_W
        return pl.BlockSpec(block, lambda i: (i // per_batch, 0, i % per_batch, 0))
    return pl.BlockSpec(block, lambda i: (i, 0, 0, 0))


def _mod_spec(mod, on_grid):
    if on_grid:
        per_batch = GRID_W // COL_TILE_W
        return pl.BlockSpec((1, 6, D_MODEL), lambda i: (i // per_batch, 0, 0))
    return pl.BlockSpec((1, 6, D_MODEL), lambda i: (0, 0, 0))


def _v_conv_geometry(on_grid):
    return (TOKENS_PER_TILE, COL_TILE_W) if on_grid else (256, 1)


def _mixer_out(x2d, o_f, o_b, hc, cgxh, mod, p, *, on_grid, batch):
    tokens = x2d.shape[0]
    seg, stride = _v_conv_geometry(on_grid)
    pad_rows = CONV_GAP + (TOKENS_PER_TILE // seg) * (seg + CONV_GAP)
    consts = [p["g_pre_mix"], p["g_post_mix"], p["dn_norm_g"], p["w_z"], p["w_bg"], p["w_mgate"],
              p["w_dn_out"], p["w_cf_out"], p["w_sc_out"], p["w_o"], p["sc_conv"]]
    tiled = [x2d, o_f, o_b, hc, cgxh]
    out = pl.pallas_call(
        functools.partial(_mixer_out_kernel, seg=seg, stride=stride),
        out_shape=jax.ShapeDtypeStruct(_tile_view(x2d, on_grid, batch).shape, F32),
        grid=(tokens // TOKENS_PER_TILE,),
        in_specs=[_tile_spec(a.shape[1], on_grid) for a in tiled] + [_mod_spec(mod, on_grid)]
                 + [_const_spec(c.shape) for c in consts],
        out_specs=_tile_spec(D_MODEL, on_grid),
        scratch_shapes=[pltpu.VMEM((pad_rows, SC_WIDTH), F32), pltpu.VMEM((TOKENS_PER_TILE, D_MODEL), BF16)],
        compiler_params=pltpu.CompilerParams(dimension_semantics=("parallel",),
                                             vmem_limit_bytes=VMEM_LIMIT_BYTES),
        name="mixer_out",
    )(*[_tile_view(a, on_grid, batch) for a in tiled], mod, *consts)
    return out.reshape(tokens, D_MODEL)


def _ffn(x2d, mod, p, *, on_grid, batch):
    tokens = x2d.shape[0]
    seg, stride = _v_conv_geometry(on_grid)
    pad_rows = CONV_GAP + (TOKENS_PER_TILE // seg) * (seg + CONV_GAP)
    consts = [p["g_pre_ffn"], p["g_post_ffn"], p["w_ffn_up"], p["w_ffn_down"], p["ffn_conv"]]
    out = pl.pallas_call(
        functools.partial(_ffn_kernel, seg=seg, stride=stride),
        out_shape=jax.ShapeDtypeStruct(_tile_view(x2d, on_grid, batch).shape, F32),
        grid=(tokens // TOKENS_PER_TILE,),
        in_specs=[_tile_spec(D_MODEL, on_grid), _mod_spec(mod, on_grid)] + [_const_spec(c.shape) for c in consts],
        out_specs=_tile_spec(D_MODEL, on_grid),
        scratch_shapes=[pltpu.VMEM((pad_rows, FF_BLOCK), F32), pltpu.VMEM((pad_rows, FF_BLOCK), F32)],
        compiler_params=pltpu.CompilerParams(dimension_semantics=("parallel",),
                                             vmem_limit_bytes=VMEM_LIMIT_BYTES),
        name="ffn",
    )(_tile_view(x2d, on_grid, batch), mod, *consts)
    return out.reshape(tokens, D_MODEL)


def _layer_params(l, g_pre_mix, g_post_mix, g_pre_ffn, g_post_ffn, w_in, dn_conv, dn_a_log, dn_dt_bias,
                  dn_norm_g, w_dn_out, cf_conv, cf_ln_g, cf_ln_b, w_cf_out, sc_conv, w_sc_out, w_o,
                  w_ffn_up, ffn_conv, w_ffn_down):
    w = w_in[l].astype(BF16)
    row = lambda v: v.reshape(1, -1).astype(F32)
    return {
        "g_pre_mix": row(g_pre_mix[l]), "g_post_mix": row(g_post_mix[l]),
        "g_pre_ffn": row(g_pre_ffn[l]), "g_post_ffn": row(g_post_ffn[l]),
        "w_qkv": w[:, _OFF_Q:_OFF_Z], "w_z": w[:, _OFF_Z:_OFF_A], "w_dngate": w[:, _OFF_A:_OFF_CF],
        "w_cf": w[:, _OFF_CF:_OFF_SC], "w_bg": w[:, _OFF_SC:_OFF_SC + SC_WIDTH],
        "w_sc": w[:, _OFF_SC + SC_WIDTH:_OFF_GATE], "w_mgate": w[:, _OFF_GATE:],
        "dn_conv": dn_conv[l], "dn_a_log": row(dn_a_log[l]), "dn_dt_bias": row(dn_dt_bias[l]),
        "dn_norm_g": row(dn_norm_g[l]), "w_dn_out": w_dn_out[l].astype(BF16),
        "cf_conv": cf_conv[l], "cf_ln_g": row(cf_ln_g[l]), "cf_ln_b": row(cf_ln_b[l]),
        "w_cf_out": w_cf_out[l].astype(BF16), "sc_conv": sc_conv[l], "w_sc_out": w_sc_out[l].astype(BF16),
        "w_o": w_o[l].astype(BF16), "w_ffn_up": w_ffn_up[l].astype(BF16), "ffn_conv": ffn_conv[l],
        "w_ffn_down": w_ffn_down[l].astype(BF16),
    }


def _trunk_layer(x2d, mod, p, s0, *, batch, seq_len, on_grid):
    seg_h = GRID_W if on_grid else seq_len
    qkv, pack, hc, cgxh = _mixer_in(x2d, mod, p, seq_len=seq_len, seg=seg_h)
    chunks = min(seq_len // DN_CHUNK, 8)
    o_f, o_b, states = _deltanet(qkv, pack, s0, batch=batch, seq_len=seq_len, chunks=chunks)
    x2d = _mixer_out(x2d, o_f, o_b, hc, cgxh, mod, p, on_grid=on_grid, batch=batch)
    x2d = _ffn(x2d, mod, p, on_grid=on_grid, batch=batch)
    return x2d, states


def kernel(x_prompt, x_sample, state_dn, c, c_ctx, w_mod, b_mod, g_pre_mix, g_post_mix, g_pre_ffn, g_post_ffn,
           w_in, dn_conv, dn_a_log, dn_dt_bias, dn_norm_g, w_dn_out, cf_conv, cf_ln_g, cf_ln_b, w_cf_out,
           sc_conv, w_sc_out, w_o, w_ffn_up, ffn_conv, w_ffn_down):
    batch, seq, _ = x_prompt.shape
    dec_batch, dec_seq, _ = x_sample.shape
    assert dec_batch + 1 <= 8
    c_all = jnp.concatenate([c, c_ctx[None, :], jnp.zeros((8 - dec_batch - 1, D_MODEL), F32)], axis=0)
    mods = _modulation(c_all, w_mod, b_mod)
    xp = x_prompt.reshape(batch * seq, D_MODEL)
    xs = x_sample.reshape(dec_batch * dec_seq, D_MODEL)
    ctx_states = []
    for l in range(DEPTH):
        p = _layer_params(l, g_pre_mix, g_post_mix, g_pre_ffn, g_post_ffn, w_in, dn_conv, dn_a_log, dn_dt_bias,
                          dn_norm_g, w_dn_out, cf_conv, cf_ln_g, cf_ln_b, w_cf_out, sc_conv, w_sc_out, w_o,
                          w_ffn_up, ffn_conv, w_ffn_down)
        mod_lat = mods[l, :dec_batch].reshape(dec_batch, 6, D_MODEL)
        mod_ctx = mods[l, dec_batch:dec_batch + 1].reshape(1, 6, D_MODEL)
        xp, st = _trunk_layer(xp, mod_ctx, p, None, batch=batch, seq_len=seq, on_grid=False)
        ctx_states.append(st)
        xs, _ = _trunk_layer(xs, mod_lat, p, state_dn[:, l].astype(F32), batch=dec_batch, seq_len=dec_seq,
                             on_grid=True)
    new_state = jnp.stack(ctx_states, axis=1).astype(x_prompt.dtype)
    return (xp.reshape(batch, seq, D_MODEL), xs.reshape(dec_batch, dec_seq, D_MODEL), new_state)
```

```python
import functools

import numpy as np
import jax
import jax.numpy as jnp
from jax import lax
from jax.experimental import pallas as pl
from jax.experimental.pallas import tpu as pltpu

F32 = jnp.float32
BF16 = jnp.bfloat16

D_MODEL = 1024
DEPTH = 2
GRID_W = 64
DN_HEADS = 4
DN_DK = 128
DN_DV = 128
DN_BLOCK = 256
DN_BATCH_PER_STEP = 1
DN_QK = DN_HEADS * DN_DK
DN_V = DN_HEADS * DN_DV
CF_WIDTH = 512
CF_CONV_W = 31
SC_WIDTH = 512
D_FF = 2816
EPS = 1e-6

_OFF_Q = 0
_OFF_Z = 2 * DN_QK + DN_V
_OFF_A = _OFF_Z + DN_V
_OFF_CF = _OFF_A + 4 * DN_HEADS
_OFF_SC = _OFF_CF + 2 * CF_WIDTH
_OFF_GATE = _OFF_SC + 3 * SC_WIDTH

TOKENS_PER_TILE = 512
COL_TILE_W = 8
CONV_GAP = 16
ROW_BLOCK = 64
FF_BLOCK = 256
PACK_W = 128
VMEM_LIMIT_BYTES = 56 * 1024 * 1024


def _sigmoid(x):
    return 0.5 * jnp.tanh(0.5 * x) + 0.5


def _silu(x):
    return x * _sigmoid(x)


def _softplus(x):
    return jnp.maximum(x, 0.0) + jnp.log1p(jnp.exp(-jnp.abs(x)))


def _mm(a, b):
    return jnp.dot(a.astype(BF16), b.astype(BF16), preferred_element_type=F32)


def _mm_nt(a, b):
    return lax.dot_general(a.astype(BF16), b.astype(BF16), (((1,), (1,)), ((), ())),
                           preferred_element_type=F32)


def _modulated_rmsnorm(x, gain, shift, scale):
    y = x * lax.rsqrt(jnp.mean(x * x, axis=-1, keepdims=True) + EPS)
    return (y * gain) * (1.0 + scale) + shift


def _rmsnorm(x, gain):
    return x * lax.rsqrt(jnp.mean(x * x, axis=-1, keepdims=True) + EPS) * gain


def _store_segments(pad_ref, val, seg, gap):
    rows, width = val.shape
    zeros = jnp.zeros((gap, width), F32)
    pad_ref[pl.ds(0, gap), pl.ds(0, width)] = zeros
    for s in range(rows // seg):
        base = gap + s * (seg + gap)
        pad_ref[pl.ds(base, seg), pl.ds(0, width)] = val[s * seg:(s + 1) * seg]
        pad_ref[pl.ds(base + seg, gap), pl.ds(0, width)] = zeros


def _conv_block(pad_ref, w_ref, seg, gap, stride, s, r0, rows, c0, cols, w_c0):
    taps = w_ref.shape[0]
    base = gap + s * (seg + gap) + r0
    acc = None
    for k in range(taps):
        off = base + (k - taps // 2) * stride
        term = pad_ref[pl.ds(off, rows), pl.ds(c0, cols)] * w_ref[pl.ds(k, 1), pl.ds(w_c0, cols)]
        acc = term if acc is None else acc + term
    return acc


def _mod_kernel(c_ref, w_ref, b_ref, o_ref):
    s = _silu(c_ref[...])
    o_ref[0] = _mm(s, w_ref[0]) + b_ref[0]


def _modulation(c_all, w_mod, b_mod):
    tn = 1024
    n_cols = 6 * D_MODEL
    return pl.pallas_call(
        _mod_kernel,
        out_shape=jax.ShapeDtypeStruct((DEPTH, 8, n_cols), F32),
        grid=(DEPTH, n_cols // tn),
        in_specs=[
            pl.BlockSpec((8, D_MODEL), lambda l, j: (0, 0)),
            pl.BlockSpec((1, D_MODEL, tn), lambda l, j: (l, 0, j)),
            pl.BlockSpec((1, 1, tn), lambda l, j: (l, 0, j)),
        ],
        out_specs=pl.BlockSpec((1, 8, tn), lambda l, j: (l, 0, j)),
        compiler_params=pltpu.CompilerParams(dimension_semantics=("parallel", "parallel"),
                                             vmem_limit_bytes=VMEM_LIMIT_BYTES),
        name="modulation",
    )(c_all, w_mod, b_mod.reshape(DEPTH, 1, n_cols))


def _mixer_in_kernel(x_ref, mod_ref, gpre_ref, wqkv_ref, wgate_ref, wcf_ref, wsc_ref,
                     dnconv_ref, cfconv_ref, lng_ref, lnb_ref, alog_ref, dtb_ref,
                     trif_ref, trib_ref, ones_ref,
                     qkv_out, pack_out, hc_out, cgxh_out,
                     pad_ref, *, seg):
    tm = x_ref.shape[0]
    nseg = tm // seg
    gap = CONV_GAP
    hb = _modulated_rmsnorm(x_ref[...], gpre_ref[...], mod_ref[0, 0:1, :], mod_ref[0, 1:2, :]).astype(BF16)

    for grp in range(3):
        proj = jnp.dot(hb, wqkv_ref[:, grp * DN_QK:(grp + 1) * DN_QK], preferred_element_type=F32)
        _store_segments(pad_ref, proj, seg, gap)
        for s in range(nseg):
            for r0 in range(0, seg, ROW_BLOCK):
                act = _silu(_conv_block(pad_ref, dnconv_ref, seg, gap, 1, s, r0, ROW_BLOCK, 0, DN_QK,
                                        grp * DN_QK))
                rows = pl.ds(s * seg + r0, ROW_BLOCK)
                if grp == 2:
                    qkv_out[rows, pl.ds(2 * DN_QK, DN_V)] = act
                else:
                    post = DN_DK ** -0.5 if grp == 0 else 1.0
                    for hd in range(DN_HEADS):
                        a = act[:, hd * DN_DK:(hd + 1) * DN_DK]
                        a = a * lax.rsqrt(jnp.sum(a * a, axis=-1, keepdims=True) + EPS)
                        if grp == 0:
                            a = a * post
                        qkv_out[rows, pl.ds(grp * DN_QK + hd * DN_DK, DN_DK)] = a

    nh2 = 2 * DN_HEADS
    ab = jnp.dot(hb, wgate_ref[...], preferred_element_type=F32)
    g = -jnp.exp(alog_ref[...]) * _softplus(ab[:, 0:nh2] + dtb_ref[...])
    beta = _sigmoid(ab[:, nh2:2 * nh2])
    g1 = g.astype(BF16)
    r1 = g - g1.astype(F32)
    g2 = r1.astype(BF16)
    g3 = (r1 - g2.astype(F32)).astype(BF16)
    pieces = jnp.concatenate([g1, g2, g3], axis=1)

    def summed(mat_ref):
        c = jnp.dot(mat_ref[...], pieces, preferred_element_type=F32)
        return c[:, 0:nh2] + c[:, nh2:2 * nh2] + c[:, 2 * nh2:3 * nh2]

    lane = lax.broadcasted_iota(jnp.int32, (tm, nh2), 1)
    gam = jnp.where(lane < DN_HEADS, summed(trif_ref), summed(trib_ref))
    tot = summed(ones_ref)
    expg = jnp.exp(gam)
    pack_out[...] = jnp.concatenate(
        [gam, beta, expg, beta * expg, jnp.exp(tot - gam), jnp.exp(tot),
         jnp.zeros((tm, PACK_W - 6 * nh2), F32)], axis=1)

    pc = jnp.dot(hb, wcf_ref[...], preferred_element_type=F32)
    glu = pc[:, :CF_WIDTH] * _sigmoid(pc[:, CF_WIDTH:])
    _store_segments(pad_ref, glu, seg, gap)
    for s in range(nseg):
        for r0 in range(0, seg, ROW_BLOCK):
            c = _conv_block(pad_ref, cfconv_ref, seg, gap, 1, s, r0, ROW_BLOCK, 0, CF_WIDTH, 0)
            mu = jnp.mean(c, axis=-1, keepdims=True)
            cc = c - mu
            y = cc * lax.rsqrt(jnp.mean(cc * cc, axis=-1, keepdims=True) + EPS)
            y = y * lng_ref[...] + lnb_ref[...]
            hc_out[pl.ds(s * seg + r0, ROW_BLOCK), :] = _silu(y)

    ps = jnp.dot(hb, wsc_ref[...], preferred_element_type=F32)
    cgxh_out[...] = ps[:, :SC_WIDTH] * ps[:, SC_WIDTH:]


def _chunk_matrices(tm):
    idx = np.arange(tm)
    same = (idx[:, None] // DN_BLOCK) == (idx[None, :] // DN_BLOCK)
    lower = same & (idx[:, None] >= idx[None, :])
    upper = same & (idx[:, None] <= idx[None, :])
    as_bf16 = lambda m: jnp.asarray(m.astype(np.float32), dtype=BF16)
    return as_bf16(lower), as_bf16(upper), as_bf16(same)


def _const_spec(shape):
    nd = len(shape)
    return pl.BlockSpec(shape, lambda *_: (0,) * nd, pipeline_mode=pl.Buffered(1))


def _mixer_in(x2d, mod, p, *, seq_len, seg):
    tokens = x2d.shape[0]
    tm = TOKENS_PER_TILE
    tiles_per_mod = max(seq_len // tm, 1) if mod.shape[0] > 1 else None
    mod_map = (lambda i: (i // tiles_per_mod, 0, 0)) if tiles_per_mod else (lambda i: (0, 0, 0))
    trif, trib, ones = _chunk_matrices(tm)
    pad_rows = CONV_GAP + (tm // seg) * (seg + CONV_GAP)
    row = lambda i: (i, 0)
    consts = [p["g_pre_mix"], p["w_qkv"], p["w_dngate"], p["w_cf"], p["w_sc"], p["dn_conv"], p["cf_conv"],
              p["cf_ln_g"], p["cf_ln_b"], p["dn_a_log"], p["dn_dt_bias"], trif, trib, ones]
    return pl.pallas_call(
        functools.partial(_mixer_in_kernel, seg=seg),
        out_shape=(jax.ShapeDtypeStruct((tokens, 2 * DN_QK + DN_V), F32),
                   jax.ShapeDtypeStruct((tokens, PACK_W), F32),
                   jax.ShapeDtypeStruct((tokens, CF_WIDTH), F32),
                   jax.ShapeDtypeStruct((tokens, SC_WIDTH), F32)),
        grid=(tokens // tm,),
        in_specs=[pl.BlockSpec((tm, D_MODEL), row), pl.BlockSpec((1, 6, D_MODEL), mod_map)]
                 + [_const_spec(c.shape) for c in consts],
        out_specs=(pl.BlockSpec((tm, 2 * DN_QK + DN_V), row), pl.BlockSpec((tm, PACK_W), row),
                   pl.BlockSpec((tm, CF_WIDTH), row), pl.BlockSpec((tm, SC_WIDTH), row)),
        scratch_shapes=[pltpu.VMEM((pad_rows, DN_QK), F32)],
        compiler_params=pltpu.CompilerParams(dimension_semantics=("parallel",),
                                             vmem_limit_bytes=VMEM_LIMIT_BYTES),
        name="mixer_in",
    )(x2d, mod, *consts)


def _substitution_masks():
    n = DN_BLOCK
    i = np.arange(n)[:, None]
    j = np.arange(n)[None, :]
    out = []
    for strict in (i > j, i < j):
        levels = [strict & (i // 2 == j // 2)]
        size = 2
        while size < n:
            levels.append(strict & (i // (2 * size) == j // (2 * size)) & (i // size != j // size))
            size *= 2
        out.append(np.stack(levels))
    return jnp.asarray(np.stack(out).astype(np.float32), dtype=BF16)


def _deltanet_kernel(qf_ref, qb_ref, pf_ref, pb_ref, masks_ref, *rest, has_s0, nbatch):
    if has_s0:
        s0_ref, of_ref, ob_ref, sout_ref, state_ref = rest
    else:
        of_ref, ob_ref, sout_ref, state_ref = rest
    n = pl.program_id(1)
    last = pl.num_programs(1) - 1
    c = DN_BLOCK
    nh2 = 2 * DN_HEADS
    n_levels = masks_ref.shape[1]

    @pl.when(n == 0)
    def _():
        if has_s0:
            state_ref[...] = s0_ref[...]
        else:
            state_ref[...] = jnp.zeros(state_ref.shape, F32)

    ii = lax.broadcasted_iota(jnp.int32, (c, c), 0)
    jj = lax.broadcasted_iota(jnp.int32, (c, c), 1)
    eye = (ii == jj).astype(F32).astype(BF16)
    incl = ((ii >= jj), (ii <= jj))
    strict = ((ii > jj), (ii < jj))
    q_refs, p_refs, o_refs = (qf_ref, qb_ref), (pf_ref, pb_ref), (of_ref, ob_ref)

    probs = [(bi, d, hd) for bi in range(nbatch) for d in (0, 1) for hd in range(DN_HEADS)]
    pk = {(bi, d): p_refs[d][bi] for bi in range(nbatch) for d in (0, 1)}
    pk_t = {key: val.T for key, val in pk.items()}

    def col(p, grp):
        bi, d, hd = p
        j = grp * nh2 + d * DN_HEADS + hd
        return pk[(bi, d)][:, j:j + 1]

    def head(p, off):
        bi, d, hd = p
        return q_refs[d][bi, :, pl.ds(off + hd * DN_DK, DN_DK)]

    q = [head(p, 0) for p in probs]
    k = [head(p, DN_QK) for p in probs]
    v = [head(p, 2 * DN_QK) for p in probs]
    kb = [x.astype(BF16) for x in k]
    kk = [_mm_nt(x, x) for x in kb]
    e = []
    for p in probs:
        bi, d, hd = p
        gam_r = pk_t[(bi, d)][d * DN_HEADS + hd:d * DN_HEADS + hd + 1, :]
        e.append(jnp.exp(jnp.where(incl[d], col(p, 0) - gam_r, 0.0)))
    a = [jnp.where(strict[p[1]], col(p, 1) * kk_i * e_i, 0.0).astype(BF16) for p, kk_i, e_i in zip(probs, kk, e)]
    inv = [eye - a_i * masks_ref[p[1], 0] for p, a_i in zip(probs, a)]
    for lvl in range(1, n_levels):
        off = [a_i * masks_ref[p[1], lvl] for p, a_i in zip(probs, a)]
        t = [jnp.dot(inv_i, off_i, preferred_element_type=F32).astype(BF16) for inv_i, off_i in zip(inv, off)]
        inv = [inv_i - jnp.dot(t_i, inv_i, preferred_element_type=F32).astype(BF16) for inv_i, t_i in zip(inv, t)]
    rhs = [jnp.concatenate([col(p, 1) * v_i, col(p, 3) * k_i], axis=1).astype(BF16)
           for p, v_i, k_i in zip(probs, v, k)]
    sol = [jnp.dot(inv_i, rhs_i, preferred_element_type=F32) for inv_i, rhs_i in zip(inv, rhs)]
    qkd = [(_mm_nt(q_i, kb_i) * jnp.where(incl[p[1]], e_i, 0.0)).astype(BF16)
           for p, q_i, kb_i, e_i in zip(probs, q, kb, e)]

    s = [state_ref[p] for p in probs]
    sb = [x.astype(BF16) for x in s]
    v_new = [(sol_i[:, :DN_DV] - jnp.dot(sol_i[:, DN_DV:].astype(BF16), sb_i, preferred_element_type=F32)).astype(BF16)
             for sol_i, sb_i in zip(sol, sb)]
    for p, q_i, sb_i, qkd_i, vn_i in zip(probs, q, sb, qkd, v_new):
        bi, d, hd = p
        o = _mm(q_i * col(p, 2), sb_i) + jnp.dot(qkd_i, vn_i, preferred_element_type=F32)
        o_refs[d][bi, :, pl.ds(hd * DN_DV, DN_DV)] = o
    for p, k_i, s_i, vn_i in zip(probs, k, s, v_new):
        bi, d, hd = p
        j = 5 * nh2 + d * DN_HEADS + hd
        etot = pk[(bi, d)][0:1, j:j + 1]
        state_ref[p] = s_i * etot + _mm((k_i * col(p, 4)).T, vn_i)

    @pl.when(n == last)
    def _():
        sout_ref[...] = state_ref[...]


def _deltanet(qkv, pack, s0, *, batch, seq_len):
    tokens = qkv.shape[0]
    nb = seq_len // DN_BLOCK
    nbatch = DN_BATCH_PER_STEP
    width = 2 * DN_QK + DN_V
    fwd = lambda b, n: (b, n, 0)
    bwd = lambda b, n: (b, nb - 1 - n, 0)
    state_block = (nbatch, 2, DN_HEADS, DN_DK, DN_DV)
    state_map = lambda b, n: (b, 0, 0, 0, 0)
    masks = _substitution_masks()
    qkv3 = qkv.reshape(batch, seq_len, width)
    pack3 = pack.reshape(batch, seq_len, PACK_W)
    in_specs = [pl.BlockSpec((nbatch, DN_BLOCK, width), fwd), pl.BlockSpec((nbatch, DN_BLOCK, width), bwd),
                pl.BlockSpec((nbatch, DN_BLOCK, PACK_W), fwd), pl.BlockSpec((nbatch, DN_BLOCK, PACK_W), bwd),
                _const_spec(masks.shape)]
    args = [qkv3, qkv3, pack3, pack3, masks]
    if s0 is not None:
        in_specs.append(pl.BlockSpec(state_block, state_map))
        args.append(s0)
    o_f, o_b, states = pl.pallas_call(
        functools.partial(_deltanet_kernel, has_s0=s0 is not None, nbatch=nbatch),
        out_shape=(jax.ShapeDtypeStruct((batch, seq_len, DN_V), F32),
                   jax.ShapeDtypeStruct((batch, seq_len, DN_V), F32),
                   jax.ShapeDtypeStruct((batch, 2, DN_HEADS, DN_DK, DN_DV), F32)),
        grid=(batch // nbatch, nb),
        in_specs=in_specs,
        out_specs=(pl.BlockSpec((nbatch, DN_BLOCK, DN_V), fwd), pl.BlockSpec((nbatch, DN_BLOCK, DN_V), bwd),
                   pl.BlockSpec(state_block, state_map)),
        scratch_shapes=[pltpu.VMEM(state_block, F32)],
        compiler_params=pltpu.CompilerParams(dimension_semantics=("parallel", "arbitrary"),
                                             vmem_limit_bytes=VMEM_LIMIT_BYTES),
        name="deltanet",
    )(*args)
    return o_f.reshape(tokens, DN_V), o_b.reshape(tokens, DN_V), states


def _mixer_out_kernel(x_ref, of_ref, ob_ref, hc_ref, cg_ref, mod_ref, gpre_ref, gpost_ref, dnn_ref,
                      wz_ref, wbg_ref, wmg_ref, wdn_ref, wcf_ref, wsc_ref, wo_ref, scconv_ref,
                      out_ref, pad_ref, min_ref, *, seg, stride):
    tm = TOKENS_PER_TILE
    gap = CONV_GAP
    x = x_ref[...].reshape(tm, D_MODEL)
    hb = _modulated_rmsnorm(x, gpre_ref[...], mod_ref[0, 0:1, :], mod_ref[0, 1:2, :]).astype(BF16)

    o = of_ref[...].reshape(tm, DN_V) + ob_ref[...].reshape(tm, DN_V)
    z = jnp.dot(hb, wz_ref[...], preferred_element_type=F32)
    heads = []
    for hd in range(DN_HEADS):
        oh = _rmsnorm(o[:, hd * DN_DV:(hd + 1) * DN_DV], dnn_ref[...])
        heads.append((oh * _silu(z[:, hd * DN_DV:(hd + 1) * DN_DV])).astype(BF16))
    o_gated = jnp.concatenate(heads, axis=1)

    _store_segments(pad_ref, cg_ref[...].reshape(tm, SC_WIDTH), seg, gap)
    bg = jnp.dot(hb, wbg_ref[...], preferred_element_type=F32)
    sc_parts = []
    for s in range(tm // seg):
        for r0 in range(0, seg, ROW_BLOCK):
            conv = _conv_block(pad_ref, scconv_ref, seg, gap, stride, s, r0, ROW_BLOCK, 0, SC_WIDTH, 0)
            sc_parts.append(conv)
    sc_in = (bg * jnp.concatenate(sc_parts, axis=0)).astype(BF16)
    hc = hc_ref[...].reshape(tm, CF_WIDTH).astype(BF16)

    cb = 256
    for c0 in range(0, D_MODEL, cb):
        cols = slice(c0, c0 + cb)
        y_dn = jnp.dot(o_gated, wdn_ref[:, cols], preferred_element_type=F32)
        y_cf = jnp.dot(hc, wcf_ref[:, cols], preferred_element_type=F32)
        y_sc = jnp.dot(sc_in, wsc_ref[:, cols], preferred_element_type=F32)
        ga = _sigmoid(jnp.dot(hb, wmg_ref[:, c0:c0 + cb], preferred_element_type=F32))
        gb = _sigmoid(jnp.dot(hb, wmg_ref[:, D_MODEL + c0:D_MODEL + c0 + cb], preferred_element_type=F32))
        gc = _sigmoid(jnp.dot(hb, wmg_ref[:, 2 * D_MODEL + c0:2 * D_MODEL + c0 + cb], preferred_element_type=F32))
        min_ref[:, cols] = (ga * y_dn + gb * y_cf + gc * y_sc).astype(BF16)

    m = jnp.dot(min_ref[...], wo_ref[...], preferred_element_type=F32)
    out = x + mod_ref[0, 2:3, :] * _rmsnorm(m, gpost_ref[...])
    out_ref[...] = out.reshape(out_ref.shape)


def _ffn_kernel(x_ref, mod_ref, gpre_ref, gpost_ref, wup_ref, wdown_ref, conv_ref,
                out_ref, pad_a, pad_b, *, seg, stride):
    tm = TOKENS_PER_TILE
    gap = CONV_GAP
    x = x_ref[...].reshape(tm, D_MODEL)
    hb = _modulated_rmsnorm(x, gpre_ref[...], mod_ref[0, 3:4, :], mod_ref[0, 4:5, :]).astype(BF16)
    y = jnp.zeros((tm, D_MODEL), F32)
    for f0 in range(0, D_FF, FF_BLOCK):
        ua = jnp.dot(hb, wup_ref[:, f0:f0 + FF_BLOCK], preferred_element_type=F32)
        ub = jnp.dot(hb, wup_ref[:, D_FF + f0:D_FF + f0 + FF_BLOCK], preferred_element_type=F32)
        _store_segments(pad_a, ua, seg, gap)
        _store_segments(pad_b, ub, seg, gap)
        parts = []
        for s in range(tm // seg):
            for r0 in range(0, seg, ROW_BLOCK):
                ca = _conv_block(pad_a, conv_ref, seg, gap, stride, s, r0, ROW_BLOCK, 0, FF_BLOCK, f0)
                cb = _conv_block(pad_b, conv_ref, seg, gap, stride, s, r0, ROW_BLOCK, 0, FF_BLOCK, D_FF + f0)
                parts.append((_silu(ca) * cb).astype(BF16))
        act = jnp.concatenate(parts, axis=0)
        y = y + jnp.dot(act, wdown_ref[f0:f0 + FF_BLOCK, :], preferred_element_type=F32)
    out = x + mod_ref[0, 5:6, :] * _rmsnorm(y, gpost_ref[...])
    out_ref[...] = out.reshape(out_ref.shape)


def _tile_view(a2d, on_grid, batch):
    width = a2d.shape[1]
    if on_grid:
        return a2d.reshape(batch, a2d.shape[0] // (batch * GRID_W), GRID_W, width)
    return a2d.reshape(a2d.shape[0] // TOKENS_PER_TILE, TOKENS_PER_TILE // COL_TILE_W, COL_TILE_W, width)


def _tile_spec(width, on_grid):
    block = (1, TOKENS_PER_TILE // COL_TILE_W, COL_TILE_W, width)
    if on_grid:
        per_batch = GRID_W // COL_TILE_W
        return pl.BlockSpec(block, lambda i: (i // per_batch, 0, i % per_batch, 0))
    return pl.BlockSpec(block, lambda i: (i, 0, 0, 0))


def _mod_spec(mod, on_grid):
    if on_grid:
        per_batch = GRID_W // COL_TILE_W
        return pl.BlockSpec((1, 6, D_MODEL), lambda i: (i // per_batch, 0, 0))
    return pl.BlockSpec((1, 6, D_MODEL), lambda i: (0, 0, 0))


def _v_conv_geometry(on_grid):
    return (TOKENS_PER_TILE, COL_TILE_W) if on_grid else (256, 1)


def _mixer_out(x2d, o_f, o_b, hc, cgxh, mod, p, *, on_grid, batch):
    tokens = x2d.shape[0]
    seg, stride = _v_conv_geometry(on_grid)
    pad_rows = CONV_GAP + (TOKENS_PER_TILE // seg) * (seg + CONV_GAP)
    consts = [p["g_pre_mix"], p["g_post_mix"], p["dn_norm_g"], p["w_z"], p["w_bg"], p["w_mgate"],
              p["w_dn_out"], p["w_cf_out"], p["w_sc_out"], p["w_o"], p["sc_conv"]]
    tiled = [x2d, o_f, o_b, hc, cgxh]
    out = pl.pallas_call(
        functools.partial(_mixer_out_kernel, seg=seg, stride=stride),
        out_shape=jax.ShapeDtypeStruct(_tile_view(x2d, on_grid, batch).shape, F32),
        grid=(tokens // TOKENS_PER_TILE,),
        in_specs=[_tile_spec(a.shape[1], on_grid) for a in tiled] + [_mod_spec(mod, on_grid)]
                 + [_const_spec(c.shape) for c in consts],
        out_specs=_tile_spec(D_MODEL, on_grid),
        scratch_shapes=[pltpu.VMEM((pad_rows, SC_WIDTH), F32), pltpu.VMEM((TOKENS_PER_TILE, D_MODEL), BF16)],
        compiler_params=pltpu.CompilerParams(dimension_semantics=("parallel",),
                                             vmem_limit_bytes=VMEM_LIMIT_BYTES),
        name="mixer_out",
    )(*[_tile_view(a, on_grid, batch) for a in tiled], mod, *consts)
    return out.reshape(tokens, D_MODEL)


def _ffn(x2d, mod, p, *, on_grid, batch):
    tokens = x2d.shape[0]
    seg, stride = _v_conv_geometry(on_grid)
    pad_rows = CONV_GAP + (TOKENS_PER_TILE // seg) * (seg + CONV_GAP)
    consts = [p["g_pre_ffn"], p["g_post_ffn"], p["w_ffn_up"], p["w_ffn_down"], p["ffn_conv"]]
    out = pl.pallas_call(
        functools.partial(_ffn_kernel, seg=seg, stride=stride),
        out_shape=jax.ShapeDtypeStruct(_tile_view(x2d, on_grid, batch).shape, F32),
        grid=(tokens // TOKENS_PER_TILE,),
        in_specs=[_tile_spec(D_MODEL, on_grid), _mod_spec(mod, on_grid)] + [_const_spec(c.shape) for c in consts],
        out_specs=_tile_spec(D_MODEL, on_grid),
        scratch_shapes=[pltpu.VMEM((pad_rows, FF_BLOCK), F32), pltpu.VMEM((pad_rows, FF_BLOCK), F32)],
        compiler_params=pltpu.CompilerParams(dimension_semantics=("parallel",),
                                             vmem_limit_bytes=VMEM_LIMIT_BYTES),
        name="ffn",
    )(_tile_view(x2d, on_grid, batch), mod, *consts)
    return out.reshape(tokens, D_MODEL)


def _layer_params(l, g_pre_mix, g_post_mix, g_pre_ffn, g_post_ffn, w_in, dn_conv, dn_a_log, dn_dt_bias,
                  dn_norm_g, w_dn_out, cf_conv, cf_ln_g, cf_ln_b, w_cf_out, sc_conv, w_sc_out, w_o,
                  w_ffn_up, ffn_conv, w_ffn_down):
    w = w_in[l].astype(BF16)
    row = lambda v: v.reshape(1, -1).astype(F32)
    return {
        "g_pre_mix": row(g_pre_mix[l]), "g_post_mix": row(g_post_mix[l]),
        "g_pre_ffn": row(g_pre_ffn[l]), "g_post_ffn": row(g_post_ffn[l]),
        "w_qkv": w[:, _OFF_Q:_OFF_Z], "w_z": w[:, _OFF_Z:_OFF_A], "w_dngate": w[:, _OFF_A:_OFF_CF],
        "w_cf": w[:, _OFF_CF:_OFF_SC], "w_bg": w[:, _OFF_SC:_OFF_SC + SC_WIDTH],
        "w_sc": w[:, _OFF_SC + SC_WIDTH:_OFF_GATE], "w_mgate": w[:, _OFF_GATE:],
        "dn_conv": dn_conv[l], "dn_a_log": row(dn_a_log[l]), "dn_dt_bias": row(dn_dt_bias[l]),
        "dn_norm_g": row(dn_norm_g[l]), "w_dn_out": w_dn_out[l].astype(BF16),
        "cf_conv": cf_conv[l], "cf_ln_g": row(cf_ln_g[l]), "cf_ln_b": row(cf_ln_b[l]),
        "w_cf_out": w_cf_out[l].astype(BF16), "sc_conv": sc_conv[l], "w_sc_out": w_sc_out[l].astype(BF16),
        "w_o": w_o[l].astype(BF16), "w_ffn_up": w_ffn_up[l].astype(BF16), "ffn_conv": ffn_conv[l],
        "w_ffn_down": w_ffn_down[l].astype(BF16),
    }


def _trunk_layer(x2d, mod, p, s0, *, batch, seq_len, on_grid):
    seg_h = GRID_W if on_grid else seq_len
    qkv, pack, hc, cgxh = _mixer_in(x2d, mod, p, seq_len=seq_len, seg=seg_h)
    o_f, o_b, states = _deltanet(qkv, pack, s0, batch=batch, seq_len=seq_len)
    x2d = _mixer_out(x2d, o_f, o_b, hc, cgxh, mod, p, on_grid=on_grid, batch=batch)
    x2d = _ffn(x2d, mod, p, on_grid=on_grid, batch=batch)
    return x2d, states


def kernel(x_prompt, x_sample, state_dn, c, c_ctx, w_mod, b_mod, g_pre_mix, g_post_mix, g_pre_ffn, g_post_ffn,
           w_in, dn_conv, dn_a_log, dn_dt_bias, dn_norm_g, w_dn_out, cf_conv, cf_ln_g, cf_ln_b, w_cf_out,
           sc_conv, w_sc_out, w_o, w_ffn_up, ffn_conv, w_ffn_down):
    batch, seq, _ = x_prompt.shape
    dec_batch, dec_seq, _ = x_sample.shape
    assert dec_batch + 1 <= 8
    c_all = jnp.concatenate([c, c_ctx[None, :], jnp.zeros((8 - dec_batch - 1, D_MODEL), F32)], axis=0)
    mods = _modulation(c_all, w_mod, b_mod)
    xp = x_prompt.reshape(batch * seq, D_MODEL)
    xs = x_sample.reshape(dec_batch * dec_seq, D_MODEL)
    ctx_states = []
    for l in range(DEPTH):
        p = _layer_params(l, g_pre_mix, g_post_mix, g_pre_ffn, g_post_ffn, w_in, dn_conv, dn_a_log, dn_dt_bias,
                          dn_norm_g, w_dn_out, cf_conv, cf_ln_g, cf_ln_b, w_cf_out, sc_conv, w_sc_out, w_o,
                          w_ffn_up, ffn_conv, w_ffn_down)
        mod_lat = mods[l, :dec_batch].reshape(dec_batch, 6, D_MODEL)
        mod_ctx = mods[l, dec_batch:dec_batch + 1].reshape(1, 6, D_MODEL)
        xp, st = _trunk_layer(xp, mod_ctx, p, None, batch=batch, seq_len=seq, on_grid=False)
        ctx_states.append(st)
        xs, _ = _trunk_layer(xs, mod_lat, p, state_dn[:, l].astype(F32), batch=dec_batch, seq_len=dec_seq,
                             on_grid=True)
    new_state = jnp.stack(ctx_states, axis=1).astype(x_prompt.dtype)
    return (xp.reshape(batch, seq, D_MODEL), xs.reshape(dec_batch, dec_seq, D_MODEL), new_state)
```

```python
import functools

import numpy as np
import jax
import jax.numpy as jnp
from jax import lax
from jax.experimental import pallas as pl
from jax.experimental.pallas import tpu as pltpu

F32 = jnp.float32
BF16 = jnp.bfloat16

D_MODEL = 1024
DEPTH = 2
GRID_W = 64
DN_HEADS = 4
DN_DK = 128
DN_DV = 128
DN_BLOCK = 128
DN_STEP_TOKENS = 256
DN_BATCH_PER_STEP = 1
DN_QK = DN_HEADS * DN_DK
DN_V = DN_HEADS * DN_DV
CF_WIDTH = 512
CF_CONV_W = 31
SC_WIDTH = 512
D_FF = 2816
EPS = 1e-6

_OFF_Q = 0
_OFF_Z = 2 * DN_QK + DN_V
_OFF_A = _OFF_Z + DN_V
_OFF_CF = _OFF_A + 4 * DN_HEADS
_OFF_SC = _OFF_CF + 2 * CF_WIDTH
_OFF_GATE = _OFF_SC + 3 * SC_WIDTH

SUBLANES = 8
LANES = 128
TOKENS_PER_TILE = 512
COL_TILE_W = 8
CONV_GAP = 16
ROW_BLOCK = 64
FF_BLOCK = 256
PACK_W = 128
VMEM_LIMIT_BYTES = 56 * 1024 * 1024


def _sigmoid(x):
    return 0.5 * jnp.tanh(0.5 * x) + 0.5


def _silu(x):
    return x * _sigmoid(x)


def _softplus(x):
    return jnp.maximum(x, 0.0) + jnp.log1p(jnp.exp(-jnp.abs(x)))


def _mm(a, b):
    return jnp.dot(a.astype(BF16), b.astype(BF16), preferred_element_type=F32)


def _mm_nt(a, b):
    return lax.dot_general(a.astype(BF16), b.astype(BF16), (((1,), (1,)), ((), ())),
                           preferred_element_type=F32)


def _modulated_rmsnorm(x, gain, shift, scale):
    y = x * lax.rsqrt(jnp.mean(x * x, axis=-1, keepdims=True) + EPS)
    return (y * gain) * (1.0 + scale) + shift


def _rmsnorm(x, gain):
    return x * lax.rsqrt(jnp.mean(x * x, axis=-1, keepdims=True) + EPS) * gain


def _store_segments(pad_ref, val, seg, gap):
    rows, width = val.shape
    zeros = jnp.zeros((gap, width), F32)
    pad_ref[pl.ds(0, gap), pl.ds(0, width)] = zeros
    for s in range(rows // seg):
        base = gap + s * (seg + gap)
        pad_ref[pl.ds(base, seg), pl.ds(0, width)] = val[s * seg:(s + 1) * seg]
        pad_ref[pl.ds(base + seg, gap), pl.ds(0, width)] = zeros


def _conv_block(pad_ref, w_ref, base, rows, c0, cols, w_c0):
    taps = w_ref.shape[0]
    offs = [k - taps // 2 for k in range(taps)]
    halo = SUBLANES * (-(-max(offs) // SUBLANES))
    n = rows + 2 * halo
    win = pad_ref[pl.ds(base - halo, n), pl.ds(c0, cols)]
    by_residue = {}
    for k, off in enumerate(offs):
        by_residue.setdefault(off % SUBLANES, []).append((k, off // SUBLANES))
    acc = None
    for b in sorted(by_residue):
        rolled = win if b == 0 else pltpu.roll(win, n - b, axis=0)
        for k, a in by_residue[b]:
            start = halo + SUBLANES * a
            term = rolled[start:start + rows] * w_ref[pl.ds(k, 1), pl.ds(w_c0, cols)]
            acc = term if acc is None else acc + term
    return acc


def _pad_rows(u):
    zeros = jnp.zeros((SUBLANES, u.shape[1]), u.dtype)
    return jnp.concatenate([zeros, u, zeros], axis=0)


def _conv3_block(u_pad, w, seg, stride, r0, rows):
    cur = u_pad[r0 + SUBLANES:r0 + SUBLANES + rows]
    if stride == SUBLANES:
        prev = u_pad[r0:r0 + rows]
        nxt = u_pad[r0 + 2 * SUBLANES:r0 + 2 * SUBLANES + rows]
    else:
        assert stride == 1 and seg % rows == 0
        n = rows + 2 * SUBLANES
        win = u_pad[r0:r0 + n]
        prev = pltpu.roll(win, 1, axis=0)[SUBLANES:SUBLANES + rows]
        nxt = pltpu.roll(win, n - 1, axis=0)[SUBLANES:SUBLANES + rows]
        sub = lax.broadcasted_iota(jnp.int32, (SUBLANES, u_pad.shape[1]), 0)
        if r0 % seg == 0:
            prev = jnp.concatenate([jnp.where(sub == 0, 0.0, prev[:SUBLANES]), prev[SUBLANES:]], axis=0)
        if (r0 + rows) % seg == 0:
            nxt = jnp.concatenate([nxt[:-SUBLANES], jnp.where(sub == SUBLANES - 1, 0.0, nxt[-SUBLANES:])], axis=0)
    return prev * w[0:1] + cur * w[1:2] + nxt * w[2:3]


def _mod_kernel(c_ref, w_ref, b_ref, o_ref):
    s = _silu(c_ref[...])
    o_ref[0] = _mm(s, w_ref[0]) + b_ref[0]


def _modulation(c_all, w_mod, b_mod):
    tn = 1024
    n_cols = 6 * D_MODEL
    return pl.pallas_call(
        _mod_kernel,
        out_shape=jax.ShapeDtypeStruct((DEPTH, 8, n_cols), F32),
        grid=(DEPTH, n_cols // tn),
        in_specs=[
            pl.BlockSpec((8, D_MODEL), lambda l, j: (0, 0)),
            pl.BlockSpec((1, D_MODEL, tn), lambda l, j: (l, 0, j)),
            pl.BlockSpec((1, 1, tn), lambda l, j: (l, 0, j)),
        ],
        out_specs=pl.BlockSpec((1, 8, tn), lambda l, j: (l, 0, j)),
        compiler_params=pltpu.CompilerParams(dimension_semantics=("parallel", "parallel"),
                                             vmem_limit_bytes=VMEM_LIMIT_BYTES),
        name="modulation",
    )(c_all, w_mod, b_mod.reshape(DEPTH, 1, n_cols))


def _mixer_in_kernel(x_ref, mod_ref, gpre_ref, wqkv_ref, wgate_ref, wcf_ref, wsc_ref,
                     dnconv_ref, cfconv_ref, lng_ref, lnb_ref, alog_ref, dtb_ref,
                     trif_ref, trib_ref, ones_ref,
                     qkv_out, pack_out, hc_out, cgxh_out,
                     pad_ref, *, seg):
    tm = x_ref.shape[0]
    nseg = tm // seg
    gap = CONV_GAP
    hb = _modulated_rmsnorm(x_ref[...], gpre_ref[...], mod_ref[0, 0:1, :], mod_ref[0, 1:2, :]).astype(BF16)

    for grp in range(3):
        proj = jnp.dot(hb, wqkv_ref[:, grp * DN_QK:(grp + 1) * DN_QK], preferred_element_type=F32)
        _store_segments(pad_ref, proj, seg, gap)
        for s in range(nseg):
            for r0 in range(0, seg, ROW_BLOCK):
                base = gap + s * (seg + gap) + r0
                rows = pl.ds(s * seg + r0, ROW_BLOCK)
                for hd in range(DN_HEADS):
                    col = grp * DN_QK + hd * DN_DK
                    a = _silu(_conv_block(pad_ref, dnconv_ref, base, ROW_BLOCK, hd * DN_DK, DN_DK, col))
                    if grp < 2:
                        a = a * lax.rsqrt(jnp.sum(a * a, axis=-1, keepdims=True) + EPS)
                    if grp == 0:
                        a = a * DN_DK ** -0.5
                    qkv_out[rows, pl.ds(col, DN_DK)] = a

    nh2 = 2 * DN_HEADS
    ab = jnp.dot(hb, wgate_ref[...], preferred_element_type=F32)
    g = -jnp.exp(alog_ref[...]) * _softplus(ab[:, 0:nh2] + dtb_ref[...])
    beta = _sigmoid(ab[:, nh2:2 * nh2])
    g1 = g.astype(BF16)
    r1 = g - g1.astype(F32)
    g2 = r1.astype(BF16)
    g3 = (r1 - g2.astype(F32)).astype(BF16)
    pieces = jnp.concatenate([g1, g2, g3], axis=1)

    def summed(mat_ref):
        c = jnp.dot(mat_ref[...], pieces, preferred_element_type=F32)
        return c[:, 0:nh2] + c[:, nh2:2 * nh2] + c[:, 2 * nh2:3 * nh2]

    lane = lax.broadcasted_iota(jnp.int32, (tm, nh2), 1)
    gam = jnp.where(lane < DN_HEADS, summed(trif_ref), summed(trib_ref))
    tot = summed(ones_ref)
    expg = jnp.exp(gam)
    pack_out[...] = jnp.concatenate(
        [gam, beta, expg, beta * expg, jnp.exp(tot - gam), jnp.exp(tot),
         jnp.zeros((tm, PACK_W - 6 * nh2), F32)], axis=1)

    pc = jnp.dot(hb, wcf_ref[...], preferred_element_type=F32)
    glu = pc[:, :CF_WIDTH] * _sigmoid(pc[:, CF_WIDTH:])
    _store_segments(pad_ref, glu, seg, gap)
    for s in range(nseg):
        for r0 in range(0, seg, ROW_BLOCK):
            base = gap + s * (seg + gap) + r0
            c = jnp.concatenate([_conv_block(pad_ref, cfconv_ref, base, ROW_BLOCK, c0, LANES, c0)
                                 for c0 in range(0, CF_WIDTH, LANES)], axis=1)
            mu = jnp.mean(c, axis=-1, keepdims=True)
            cc = c - mu
            y = cc * lax.rsqrt(jnp.mean(cc * cc, axis=-1, keepdims=True) + EPS)
            y = y * lng_ref[...] + lnb_ref[...]
            hc_out[pl.ds(s * seg + r0, ROW_BLOCK), :] = _silu(y)

    ps = jnp.dot(hb, wsc_ref[...], preferred_element_type=F32)
    cgxh_out[...] = ps[:, :SC_WIDTH] * ps[:, SC_WIDTH:]


def _chunk_matrices(tm):
    idx = np.arange(tm)
    same = (idx[:, None] // DN_BLOCK) == (idx[None, :] // DN_BLOCK)
    lower = same & (idx[:, None] >= idx[None, :])
    upper = same & (idx[:, None] <= idx[None, :])
    as_bf16 = lambda m: jnp.asarray(m.astype(np.float32), dtype=BF16)
    return as_bf16(lower), as_bf16(upper), as_bf16(same)


def _const_spec(shape):
    nd = len(shape)
    return pl.BlockSpec(shape, lambda *_: (0,) * nd, pipeline_mode=pl.Buffered(1))


def _mixer_in(x2d, mod, p, *, seq_len, seg):
    tokens = x2d.shape[0]
    tm = TOKENS_PER_TILE
    tiles_per_mod = max(seq_len // tm, 1) if mod.shape[0] > 1 else None
    mod_map = (lambda i: (i // tiles_per_mod, 0, 0)) if tiles_per_mod else (lambda i: (0, 0, 0))
    trif, trib, ones = _chunk_matrices(tm)
    pad_rows = CONV_GAP + (tm // seg) * (seg + CONV_GAP)
    row = lambda i: (i, 0)
    consts = [p["g_pre_mix"], p["w_qkv"], p["w_dngate"], p["w_cf"], p["w_sc"], p["dn_conv"], p["cf_conv"],
              p["cf_ln_g"], p["cf_ln_b"], p["dn_a_log"], p["dn_dt_bias"], trif, trib, ones]
    return pl.pallas_call(
        functools.partial(_mixer_in_kernel, seg=seg),
        out_shape=(jax.ShapeDtypeStruct((tokens, 2 * DN_QK + DN_V), F32),
                   jax.ShapeDtypeStruct((tokens, PACK_W), F32),
                   jax.ShapeDtypeStruct((tokens, CF_WIDTH), F32),
                   jax.ShapeDtypeStruct((tokens, SC_WIDTH), F32)),
        grid=(tokens // tm,),
        in_specs=[pl.BlockSpec((tm, D_MODEL), row), pl.BlockSpec((1, 6, D_MODEL), mod_map)]
                 + [_const_spec(c.shape) for c in consts],
        out_specs=(pl.BlockSpec((tm, 2 * DN_QK + DN_V), row), pl.BlockSpec((tm, PACK_W), row),
                   pl.BlockSpec((tm, CF_WIDTH), row), pl.BlockSpec((tm, SC_WIDTH), row)),
        scratch_shapes=[pltpu.VMEM((pad_rows, DN_QK), F32)],
        compiler_params=pltpu.CompilerParams(dimension_semantics=("parallel",),
                                             vmem_limit_bytes=VMEM_LIMIT_BYTES),
        name="mixer_in",
    )(x2d, mod, *consts)


def _substitution_masks():
    n = DN_BLOCK
    i = np.arange(n)[:, None]
    j = np.arange(n)[None, :]
    out = []
    for strict in (i > j, i < j):
        levels = [strict & (i // 2 == j // 2)]
        size = 2
        while size < n:
            levels.append(strict & (i // (2 * size) == j // (2 * size)) & (i // size != j // size))
            size *= 2
        out.append(np.stack(levels))
    return jnp.asarray(np.stack(out).astype(np.float32), dtype=BF16)


def _deltanet_kernel(qf_ref, qb_ref, pf_ref, pb_ref, masks_ref, *rest, has_s0, nbatch):
    if has_s0:
        s0_ref, of_ref, ob_ref, sout_ref, state_ref = rest
    else:
        of_ref, ob_ref, sout_ref, state_ref = rest
    n = pl.program_id(1)
    last = pl.num_programs(1) - 1
    c = DN_BLOCK
    nh2 = 2 * DN_HEADS
    n_levels = masks_ref.shape[1]

    @pl.when(n == 0)
    def _():
        if has_s0:
            state_ref[...] = s0_ref[...]
        else:
            state_ref[...] = jnp.zeros(state_ref.shape, F32)

    ii = lax.broadcasted_iota(jnp.int32, (c, c), 0)
    jj = lax.broadcasted_iota(jnp.int32, (c, c), 1)
    eye = (ii == jj).astype(F32).astype(BF16)
    incl = ((ii >= jj), (ii <= jj))
    strict = ((ii > jj), (ii < jj))
    q_refs, p_refs, o_refs = (qf_ref, qb_ref), (pf_ref, pb_ref), (of_ref, ob_ref)

    nblk = DN_STEP_TOKENS // c
    chains = [(bi, d, hd) for bi in range(nbatch) for d in (0, 1) for hd in range(DN_HEADS)]
    probs = [(bi, d, blk, hd) for bi, d, hd in chains for blk in range(nblk)]
    rows = lambda blk: pl.ds(blk * c, c)
    pk = {(bi, d, blk): p_refs[d][bi, rows(blk), :]
          for bi in range(nbatch) for d in (0, 1) for blk in range(nblk)}
    pk_t = {key: val.T for key, val in pk.items()}

    def col(p, grp):
        bi, d, blk, hd = p
        j = grp * nh2 + d * DN_HEADS + hd
        return pk[(bi, d, blk)][:, j:j + 1]

    def head(p, off):
        bi, d, blk, hd = p
        return q_refs[d][bi, rows(blk), pl.ds(off + hd * DN_DK, DN_DK)]

    q = {p: head(p, 0) for p in probs}
    k = {p: head(p, DN_QK) for p in probs}
    v = {p: head(p, 2 * DN_QK) for p in probs}
    kb = {p: k[p].astype(BF16) for p in probs}
    kk = {p: _mm_nt(kb[p], kb[p]) for p in probs}
    e = {}
    for p in probs:
        bi, d, blk, hd = p
        gam_r = pk_t[(bi, d, blk)][d * DN_HEADS + hd:d * DN_HEADS + hd + 1, :]
        e[p] = jnp.exp(jnp.where(incl[d], col(p, 0) - gam_r, 0.0))
    a = {p: jnp.where(strict[p[1]], col(p, 1) * kk[p] * e[p], 0.0).astype(BF16) for p in probs}
    inv = {p: eye - a[p] * masks_ref[p[1], 0] for p in probs}
    for lvl in range(1, n_levels):
        off = {p: a[p] * masks_ref[p[1], lvl] for p in probs}
        t = {p: jnp.dot(inv[p], off[p], preferred_element_type=F32).astype(BF16) for p in probs}
        inv = {p: inv[p] - jnp.dot(t[p], inv[p], preferred_element_type=F32).astype(BF16) for p in probs}
    rhs = {p: jnp.concatenate([col(p, 1) * v[p], col(p, 3) * k[p]], axis=1).astype(BF16) for p in probs}
    sol = {p: jnp.dot(inv[p], rhs[p], preferred_element_type=F32) for p in probs}
    qkd = {p: (_mm_nt(q[p], kb[p]) * jnp.where(incl[p[1]], e[p], 0.0)).astype(BF16) for p in probs}
    q_cat = {p: jnp.concatenate([(q[p] * col(p, 2)).astype(BF16), qkd[p]], axis=1) for p in probs}
    k_dec_t = {p: (k[p] * col(p, 4)).T.astype(BF16) for p in probs}

    state = {ch: state_ref[ch] for ch in chains}
    for step in range(nblk):
        cur = {(bi, d, hd): (bi, d, step if d == 0 else nblk - 1 - step, hd) for bi, d, hd in chains}
        sb = {ch: state[ch].astype(BF16) for ch in chains}
        v_new = {ch: (sol[cur[ch]][:, :DN_DV]
                      - jnp.dot(sol[cur[ch]][:, DN_DV:].astype(BF16), sb[ch], preferred_element_type=F32)).astype(BF16)
                 for ch in chains}
        for ch in chains:
            bi, d, blk, hd = cur[ch]
            o_refs[d][bi, rows(blk), pl.ds(hd * DN_DV, DN_DV)] = jnp.dot(
                q_cat[cur[ch]], jnp.concatenate([sb[ch], v_new[ch]], axis=0), preferred_element_type=F32)
        new_state = {}
        for ch in chains:
            bi, d, blk, hd = cur[ch]
            j = 5 * nh2 + d * DN_HEADS + hd
            etot = pk[(bi, d, blk)][0:1, j:j + 1]
            new_state[ch] = state[ch] * etot + jnp.dot(k_dec_t[cur[ch]], v_new[ch], preferred_element_type=F32)
        state = new_state
    for ch in chains:
        state_ref[ch] = state[ch]

    @pl.when(n == last)
    def _():
        sout_ref[...] = state_ref[...]


def _deltanet(qkv, pack, s0, *, batch, seq_len):
    tokens = qkv.shape[0]
    step = DN_STEP_TOKENS
    nb = seq_len // step
    nbatch = DN_BATCH_PER_STEP
    width = 2 * DN_QK + DN_V
    fwd = lambda b, n: (b, n, 0)
    bwd = lambda b, n: (b, nb - 1 - n, 0)
    state_block = (nbatch, 2, DN_HEADS, DN_DK, DN_DV)
    state_map = lambda b, n: (b, 0, 0, 0, 0)
    masks = _substitution_masks()
    qkv3 = qkv.reshape(batch, seq_len, width)
    pack3 = pack.reshape(batch, seq_len, PACK_W)
    in_specs = [pl.BlockSpec((nbatch, step, width), fwd), pl.BlockSpec((nbatch, step, width), bwd),
                pl.BlockSpec((nbatch, step, PACK_W), fwd), pl.BlockSpec((nbatch, step, PACK_W), bwd),
                _const_spec(masks.shape)]
    args = [qkv3, qkv3, pack3, pack3, masks]
    if s0 is not None:
        in_specs.append(pl.BlockSpec(state_block, state_map))
        args.append(s0)
    o_f, o_b, states = pl.pallas_call(
        functools.partial(_deltanet_kernel, has_s0=s0 is not None, nbatch=nbatch),
        out_shape=(jax.ShapeDtypeStruct((batch, seq_len, DN_V), F32),
                   jax.ShapeDtypeStruct((batch, seq_len, DN_V), F32),
                   jax.ShapeDtypeStruct((batch, 2, DN_HEADS, DN_DK, DN_DV), F32)),
        grid=(batch // nbatch, nb),
        in_specs=in_specs,
        out_specs=(pl.BlockSpec((nbatch, step, DN_V), fwd), pl.BlockSpec((nbatch, step, DN_V), bwd),
                   pl.BlockSpec(state_block, state_map)),
        scratch_shapes=[pltpu.VMEM(state_block, F32)],
        compiler_params=pltpu.CompilerParams(dimension_semantics=("parallel", "arbitrary"),
                                             vmem_limit_bytes=VMEM_LIMIT_BYTES),
        name="deltanet",
    )(*args)
    return o_f.reshape(tokens, DN_V), o_b.reshape(tokens, DN_V), states


def _mixer_out_kernel(x_ref, of_ref, ob_ref, hc_ref, cg_ref, mod_ref, gpre_ref, gpost_ref, dnn_ref,
                      wz_ref, wbg_ref, wmg_ref, wdn_ref, wcf_ref, wsc_ref, wo_ref, scconv_ref,
                      out_ref, min_ref, *, seg, stride):
    tm = TOKENS_PER_TILE
    x = x_ref[...].reshape(tm, D_MODEL)
    hb = _modulated_rmsnorm(x, gpre_ref[...], mod_ref[0, 0:1, :], mod_ref[0, 1:2, :]).astype(BF16)

    o = of_ref[...].reshape(tm, DN_V) + ob_ref[...].reshape(tm, DN_V)
    z = jnp.dot(hb, wz_ref[...], preferred_element_type=F32)
    heads = []
    for hd in range(DN_HEADS):
        oh = _rmsnorm(o[:, hd * DN_DV:(hd + 1) * DN_DV], dnn_ref[...])
        heads.append((oh * _silu(z[:, hd * DN_DV:(hd + 1) * DN_DV])).astype(BF16))
    o_gated = jnp.concatenate(heads, axis=1)

    cg_pad = _pad_rows(cg_ref[...].reshape(tm, SC_WIDTH))
    bg = jnp.dot(hb, wbg_ref[...], preferred_element_type=F32)
    w_sc = scconv_ref[...]
    sc_parts = [_conv3_block(cg_pad, w_sc, seg, stride, r0, ROW_BLOCK) * bg[r0:r0 + ROW_BLOCK]
                for r0 in range(0, tm, ROW_BLOCK)]
    sc_in = jnp.concatenate(sc_parts, axis=0).astype(BF16)
    hc = hc_ref[...].reshape(tm, CF_WIDTH).astype(BF16)

    cb = 256
    for c0 in range(0, D_MODEL, cb):
        cols = slice(c0, c0 + cb)
        y_dn = jnp.dot(o_gated, wdn_ref[:, cols], preferred_element_type=F32)
        y_cf = jnp.dot(hc, wcf_ref[:, cols], preferred_element_type=F32)
        y_sc = jnp.dot(sc_in, wsc_ref[:, cols], preferred_element_type=F32)
        ga = _sigmoid(jnp.dot(hb, wmg_ref[:, c0:c0 + cb], preferred_element_type=F32))
        gb = _sigmoid(jnp.dot(hb, wmg_ref[:, D_MODEL + c0:D_MODEL + c0 + cb], preferred_element_type=F32))
        gc = _sigmoid(jnp.dot(hb, wmg_ref[:, 2 * D_MODEL + c0:2 * D_MODEL + c0 + cb], preferred_element_type=F32))
        min_ref[:, cols] = (ga * y_dn + gb * y_cf + gc * y_sc).astype(BF16)

    m = jnp.dot(min_ref[...], wo_ref[...], preferred_element_type=F32)
    out = x + mod_ref[0, 2:3, :] * _rmsnorm(m, gpost_ref[...])
    out_ref[...] = out.reshape(out_ref.shape)


def _ffn_kernel(x_ref, mod_ref, gpre_ref, gpost_ref, wup_ref, wdown_ref, conv_ref,
                out_ref, act_ref, *, seg, stride):
    tm = TOKENS_PER_TILE
    x = x_ref[...].reshape(tm, D_MODEL)
    hb = _modulated_rmsnorm(x, gpre_ref[...], mod_ref[0, 3:4, :], mod_ref[0, 4:5, :]).astype(BF16)
    for f0 in range(0, D_FF, FF_BLOCK):
        ua = _pad_rows(jnp.dot(hb, wup_ref[:, f0:f0 + FF_BLOCK], preferred_element_type=F32))
        ub = _pad_rows(jnp.dot(hb, wup_ref[:, D_FF + f0:D_FF + f0 + FF_BLOCK], preferred_element_type=F32))
        wa = conv_ref[:, f0:f0 + FF_BLOCK]
        wb = conv_ref[:, D_FF + f0:D_FF + f0 + FF_BLOCK]
        for r0 in range(0, tm, ROW_BLOCK):
            ca = _conv3_block(ua, wa, seg, stride, r0, ROW_BLOCK)
            cb = _conv3_block(ub, wb, seg, stride, r0, ROW_BLOCK)
            act_ref[pl.ds(r0, ROW_BLOCK), pl.ds(f0, FF_BLOCK)] = (_silu(ca) * cb).astype(BF16)
    y = jnp.dot(act_ref[...], wdown_ref[...], preferred_element_type=F32)
    out = x + mod_ref[0, 5:6, :] * _rmsnorm(y, gpost_ref[...])
    out_ref[...] = out.reshape(out_ref.shape)


def _tile_view(a2d, on_grid, batch):
    width = a2d.shape[1]
    if on_grid:
        return a2d.reshape(batch, a2d.shape[0] // (batch * GRID_W), GRID_W, width)
    return a2d.reshape(a2d.shape[0] // TOKENS_PER_TILE, TOKENS_PER_TILE // COL_TILE_W, COL_TILE_W, width)


def _tile_spec(width, on_grid):
    block = (1, TOKENS_PER_TILE // COL_TILE_W, COL_TILE_W, width)
    if on_grid:
        per_batch = GRID_W // COL_TILE_W
        return pl.BlockSpec(block, lambda i: (i // per_batch, 0, i % per_batch, 0))
    return pl.BlockSpec(block, lambda i: (i, 0, 0, 0))


def _mod_spec(mod, on_grid):
    if on_grid:
        per_batch = GRID_W // COL_TILE_W
        return pl.BlockSpec((1, 6, D_MODEL), lambda i: (i // per_batch, 0, 0))
    return pl.BlockSpec((1, 6, D_MODEL), lambda i: (0, 0, 0))


def _v_conv_geometry(on_grid):
    return (TOKENS_PER_TILE, COL_TILE_W) if on_grid else (256, 1)


def _mixer_out(x2d, o_f, o_b, hc, cgxh, mod, p, *, on_grid, batch):
    tokens = x2d.shape[0]
    seg, stride = _v_conv_geometry(on_grid)
    consts = [p["g_pre_mix"], p["g_post_mix"], p["dn_norm_g"], p["w_z"], p["w_bg"], p["w_mgate"],
              p["w_dn_out"], p["w_cf_out"], p["w_sc_out"], p["w_o"], p["sc_conv"]]
    tiled = [x2d, o_f, o_b, hc, cgxh]
    out = pl.pallas_call(
        functools.partial(_mixer_out_kernel, seg=seg, stride=stride),
        out_shape=jax.ShapeDtypeStruct(_tile_view(x2d, on_grid, batch).shape, F32),
        grid=(tokens // TOKENS_PER_TILE,),
        in_specs=[_tile_spec(a.shape[1], on_grid) for a in tiled] + [_mod_spec(mod, on_grid)]
                 + [_const_spec(c.shape) for c in consts],
        out_specs=_tile_spec(D_MODEL, on_grid),
        scratch_shapes=[pltpu.VMEM((TOKENS_PER_TILE, D_MODEL), BF16)],
        compiler_params=pltpu.CompilerParams(dimension_semantics=("parallel",),
                                             vmem_limit_bytes=VMEM_LIMIT_BYTES),
        name="mixer_out",
    )(*[_tile_view(a, on_grid, batch) for a in tiled], mod, *consts)
    return out.reshape(tokens, D_MODEL)


def _ffn(x2d, mod, p, *, on_grid, batch):
    tokens = x2d.shape[0]
    seg, stride = _v_conv_geometry(on_grid)
    consts = [p["g_pre_ffn"], p["g_post_ffn"], p["w_ffn_up"], p["w_ffn_down"], p["ffn_conv"]]
    out = pl.pallas_call(
        functools.partial(_ffn_kernel, seg=seg, stride=stride),
        out_shape=jax.ShapeDtypeStruct(_tile_view(x2d, on_grid, batch).shape, F32),
        grid=(tokens // TOKENS_PER_TILE,),
        in_specs=[_tile_spec(D_MODEL, on_grid), _mod_spec(mod, on_grid)] + [_const_spec(c.shape) for c in consts],
        out_specs=_tile_spec(D_MODEL, on_grid),
        scratch_shapes=[pltpu.VMEM((TOKENS_PER_TILE, D_FF), BF16)],
        compiler_params=pltpu.CompilerParams(dimension_semantics=("parallel",),
                                             vmem_limit_bytes=VMEM_LIMIT_BYTES),
        name="ffn",
    )(_tile_view(x2d, on_grid, batch), mod, *consts)
    return out.reshape(tokens, D_MODEL)


def _layer_params(l, g_pre_mix, g_post_mix, g_pre_ffn, g_post_ffn, w_in, dn_conv, dn_a_log, dn_dt_bias,
                  dn_norm_g, w_dn_out, cf_conv, cf_ln_g, cf_ln_b, w_cf_out, sc_conv, w_sc_out, w_o,
                  w_ffn_up, ffn_conv, w_ffn_down):
    w = lambda lo, hi: w_in[l, :, lo:hi].astype(BF16)
    row = lambda v: v.reshape(1, -1).astype(F32)
    return {
        "g_pre_mix": row(g_pre_mix[l]), "g_post_mix": row(g_post_mix[l]),
        "g_pre_ffn": row(g_pre_ffn[l]), "g_post_ffn": row(g_post_ffn[l]),
        "w_qkv": w(_OFF_Q, _OFF_Z), "w_z": w(_OFF_Z, _OFF_A), "w_dngate": w(_OFF_A, _OFF_CF),
        "w_cf": w(_OFF_CF, _OFF_SC), "w_bg": w(_OFF_SC, _OFF_SC + SC_WIDTH),
        "w_sc": w(_OFF_SC + SC_WIDTH, _OFF_GATE), "w_mgate": w(_OFF_GATE, _OFF_GATE + 3 * D_MODEL),
        "dn_conv": dn_conv[l], "dn_a_log": row(dn_a_log[l]), "dn_dt_bias": row(dn_dt_bias[l]),
        "dn_norm_g": row(dn_norm_g[l]), "w_dn_out": w_dn_out[l].astype(BF16),
        "cf_conv": cf_conv[l], "cf_ln_g": row(cf_ln_g[l]), "cf_ln_b": row(cf_ln_b[l]),
        "w_cf_out": w_cf_out[l].astype(BF16), "sc_conv": sc_conv[l], "w_sc_out": w_sc_out[l].astype(BF16),
        "w_o": w_o[l].astype(BF16), "w_ffn_up": w_ffn_up[l].astype(BF16), "ffn_conv": ffn_conv[l],
        "w_ffn_down": w_ffn_down[l].astype(BF16),
    }


def _trunk_layer(x2d, mod, p, s0, *, batch, seq_len, on_grid):
    seg_h = GRID_W if on_grid else seq_len
    qkv, pack, hc, cgxh = _mixer_in(x2d, mod, p, seq_len=seq_len, seg=seg_h)
    o_f, o_b, states = _deltanet(qkv, pack, s0, batch=batch, seq_len=seq_len)
    x2d = _mixer_out(x2d, o_f, o_b, hc, cgxh, mod, p, on_grid=on_grid, batch=batch)
    x2d = _ffn(x2d, mod, p, on_grid=on_grid, batch=batch)
    return x2d, states


def kernel(x_prompt, x_sample, state_dn, c, c_ctx, w_mod, b_mod, g_pre_mix, g_post_mix, g_pre_ffn, g_post_ffn,
           w_in, dn_conv, dn_a_log, dn_dt_bias, dn_norm_g, w_dn_out, cf_conv, cf_ln_g, cf_ln_b, w_cf_out,
           sc_conv, w_sc_out, w_o, w_ffn_up, ffn_conv, w_ffn_down):
    batch, seq, _ = x_prompt.shape
    dec_batch, dec_seq, _ = x_sample.shape
    assert dec_batch + 1 <= 8
    c_all = jnp.concatenate([c, c_ctx[None, :], jnp.zeros((8 - dec_batch - 1, D_MODEL), F32)], axis=0)
    mods = _modulation(c_all, w_mod, b_mod)
    xp = x_prompt.reshape(batch * seq, D_MODEL)
    xs = x_sample.reshape(dec_batch * dec_seq, D_MODEL)
    ctx_states = []
    for l in range(DEPTH):
        p = _layer_params(l, g_pre_mix, g_post_mix, g_pre_ffn, g_post_ffn, w_in, dn_conv, dn_a_log, dn_dt_bias,
                          dn_norm_g, w_dn_out, cf_conv, cf_ln_g, cf_ln_b, w_cf_out, sc_conv, w_sc_out, w_o,
                          w_ffn_up, ffn_conv, w_ffn_down)
        mod_lat = mods[l, :dec_batch].reshape(dec_batch, 6, D_MODEL)
        mod_ctx = mods[l, dec_batch:dec_batch + 1].reshape(1, 6, D_MODEL)
        xp, st = _trunk_layer(xp, mod_ctx, p, None, batch=batch, seq_len=seq, on_grid=False)
        ctx_states.append(st)
        xs, _ = _trunk_layer(xs, mod_lat, p, state_dn[:, l].astype(F32), batch=dec_batch, seq_len=dec_seq,
                             on_grid=True)
    new_state = jnp.stack(ctx_states, axis=1).astype(x_prompt.dtype)
    return (xp.reshape(batch, seq, D_MODEL), xs.reshape(dec_batch, dec_seq, D_MODEL), new_state)
```

```python
import functools

import numpy as np
import jax
import jax.numpy as jnp
from jax import lax
from jax.experimental import pallas as pl
from jax.experimental.pallas import tpu as pltpu

F32 = jnp.float32
BF16 = jnp.bfloat16

D_MODEL = 1024
DEPTH = 2
GRID_W = 64
DN_HEADS = 4
DN_DK = 128
DN_DV = 128
DN_BLOCK = 128
DN_STEP_TOKENS = 256
DN_BATCH_PER_STEP = 2
DN_QK = DN_HEADS * DN_DK
DN_V = DN_HEADS * DN_DV
CF_WIDTH = 512
CF_CONV_W = 31
SC_WIDTH = 512
D_FF = 2816
EPS = 1e-6

_OFF_Q = 0
_OFF_Z = 2 * DN_QK + DN_V
_OFF_A = _OFF_Z + DN_V
_OFF_CF = _OFF_A + 4 * DN_HEADS
_OFF_SC = _OFF_CF + 2 * CF_WIDTH
_OFF_GATE = _OFF_SC + 3 * SC_WIDTH

SUBLANES = 8
BF16_SUBLANES = 16
LANES = 128
TOKENS_PER_TILE = 512
COL_TILE_W = 8
CONV_GAP = 16
ROW_BLOCK = 64
FF_BLOCK = 256
PACK_W = 128
VMEM_LIMIT_BYTES = 56 * 1024 * 1024


def _sigmoid(x):
    return 0.5 * jnp.tanh(0.5 * x) + 0.5


def _silu(x):
    h = 0.5 * x
    return h + h * jnp.tanh(h)


def _softplus(x):
    return jnp.maximum(x, 0.0) + jnp.log1p(jnp.exp(-jnp.abs(x)))


def _mm(a, b):
    return jnp.dot(a.astype(BF16), b.astype(BF16), preferred_element_type=F32)


def _mm_nt(a, b):
    return lax.dot_general(a.astype(BF16), b.astype(BF16), (((1,), (1,)), ((), ())),
                           preferred_element_type=F32)


def _modulated_rmsnorm(x, gain, shift, scale):
    y = x * lax.rsqrt(jnp.mean(x * x, axis=-1, keepdims=True) + EPS)
    return y * (gain * (1.0 + scale)) + shift


def _rmsnorm(x, gain):
    return x * lax.rsqrt(jnp.mean(x * x, axis=-1, keepdims=True) + EPS) * gain


def _store_segments(pad_ref, val, seg, gap):
    rows, width = val.shape
    zeros = jnp.zeros((gap, width), F32)
    pad_ref[pl.ds(0, gap), pl.ds(0, width)] = zeros
    for s in range(rows // seg):
        base = gap + s * (seg + gap)
        pad_ref[pl.ds(base, seg), pl.ds(0, width)] = val[s * seg:(s + 1) * seg]
        pad_ref[pl.ds(base + seg, gap), pl.ds(0, width)] = zeros


def _conv_block(pad_ref, w_ref, base, rows, c0, cols, w_c0):
    taps = w_ref.shape[0]
    offs = [k - taps // 2 for k in range(taps)]
    halo = SUBLANES * (-(-max(offs) // SUBLANES))
    n = rows + 2 * halo
    win = pad_ref[pl.ds(base - halo, n), pl.ds(c0, cols)]
    by_residue = {}
    for k, off in enumerate(offs):
        by_residue.setdefault(off % SUBLANES, []).append((k, off // SUBLANES))
    acc = None
    for b in sorted(by_residue):
        rolled = win if b == 0 else pltpu.roll(win, n - b, axis=0)
        for k, a in by_residue[b]:
            start = halo + SUBLANES * a
            term = rolled[start:start + rows] * w_ref[pl.ds(k, 1), pl.ds(w_c0, cols)]
            acc = term if acc is None else acc + term
    return acc


def _pad_rows(u):
    zeros = jnp.zeros((SUBLANES, u.shape[1]), u.dtype)
    return jnp.concatenate([zeros, u, zeros], axis=0)


def _conv3_block(u_pad, w, seg, stride, r0, rows):
    cur = u_pad[r0 + SUBLANES:r0 + SUBLANES + rows]
    if stride == SUBLANES:
        prev = u_pad[r0:r0 + rows]
        nxt = u_pad[r0 + 2 * SUBLANES:r0 + 2 * SUBLANES + rows]
    else:
        assert stride == 1 and seg % rows == 0
        n = rows + 2 * SUBLANES
        win = u_pad[r0:r0 + n]
        prev = pltpu.roll(win, 1, axis=0)[SUBLANES:SUBLANES + rows]
        nxt = pltpu.roll(win, n - 1, axis=0)[SUBLANES:SUBLANES + rows]
        sub = lax.broadcasted_iota(jnp.int32, (SUBLANES, u_pad.shape[1]), 0)
        if r0 % seg == 0:
            prev = jnp.concatenate([jnp.where(sub == 0, 0.0, prev[:SUBLANES]), prev[SUBLANES:]], axis=0)
        if (r0 + rows) % seg == 0:
            nxt = jnp.concatenate([nxt[:-SUBLANES], jnp.where(sub == SUBLANES - 1, 0.0, nxt[-SUBLANES:])], axis=0)
    return prev * w[0:1] + cur * w[1:2] + nxt * w[2:3]


def _mod_kernel(c_ref, w_ref, b_ref, o_ref):
    s = _silu(c_ref[...])
    o_ref[0] = _mm(s, w_ref[0]) + b_ref[0]


def _modulation(c_all, w_mod, b_mod):
    tn = 1024
    n_cols = 6 * D_MODEL
    return pl.pallas_call(
        _mod_kernel,
        out_shape=jax.ShapeDtypeStruct((DEPTH, 8, n_cols), F32),
        grid=(DEPTH, n_cols // tn),
        in_specs=[
            pl.BlockSpec((8, D_MODEL), lambda l, j: (0, 0)),
            pl.BlockSpec((1, D_MODEL, tn), lambda l, j: (l, 0, j)),
            pl.BlockSpec((1, 1, tn), lambda l, j: (l, 0, j)),
        ],
        out_specs=pl.BlockSpec((1, 8, tn), lambda l, j: (l, 0, j)),
        compiler_params=pltpu.CompilerParams(dimension_semantics=("parallel", "parallel"),
                                             vmem_limit_bytes=VMEM_LIMIT_BYTES),
        name="modulation",
    )(c_all, w_mod, b_mod.reshape(DEPTH, 1, n_cols))


def _mixer_in_kernel(x_ref, mod_ref, gpre_ref, wqkv_ref, wgate_ref, wcf_ref, wsc_ref,
                     dnconv_ref, cfconv_ref, lng_ref, lnb_ref, alog_ref, dtb_ref,
                     trif_ref, trib_ref, ones_ref,
                     qkv_out, pack_out, hc_out, cgxh_out,
                     pad_ref, *, seg):
    tm = x_ref.shape[0]
    nseg = tm // seg
    gap = CONV_GAP
    hb = _modulated_rmsnorm(x_ref[...], gpre_ref[...], mod_ref[0, 0:1, :], mod_ref[0, 1:2, :]).astype(BF16)

    for grp in range(3):
        proj = jnp.dot(hb, wqkv_ref[:, grp * DN_QK:(grp + 1) * DN_QK], preferred_element_type=F32)
        _store_segments(pad_ref, proj, seg, gap)
        for s in range(nseg):
            for r0 in range(0, seg, ROW_BLOCK):
                base = gap + s * (seg + gap) + r0
                rows = pl.ds(s * seg + r0, ROW_BLOCK)
                for hd in range(DN_HEADS):
                    col = grp * DN_QK + hd * DN_DK
                    a = _silu(_conv_block(pad_ref, dnconv_ref, base, ROW_BLOCK, hd * DN_DK, DN_DK, col))
                    if grp < 2:
                        a = a * lax.rsqrt(jnp.sum(a * a, axis=-1, keepdims=True) + EPS)
                    if grp == 0:
                        a = a * DN_DK ** -0.5
                    qkv_out[rows, pl.ds(col, DN_DK)] = a

    nh2 = 2 * DN_HEADS
    ab = jnp.dot(hb, wgate_ref[...], preferred_element_type=F32)
    g = -jnp.exp(alog_ref[...]) * _softplus(ab[:, 0:nh2] + dtb_ref[...])
    beta = _sigmoid(ab[:, nh2:2 * nh2])
    g1 = g.astype(BF16)
    r1 = g - g1.astype(F32)
    g2 = r1.astype(BF16)
    g3 = (r1 - g2.astype(F32)).astype(BF16)
    pieces = jnp.concatenate([g1, g2, g3], axis=1)

    def summed(mat_ref):
        c = jnp.dot(mat_ref[...], pieces, preferred_element_type=F32)
        return c[:, 0:nh2] + c[:, nh2:2 * nh2] + c[:, 2 * nh2:3 * nh2]

    lane = lax.broadcasted_iota(jnp.int32, (tm, nh2), 1)
    gam = jnp.where(lane < DN_HEADS, summed(trif_ref), summed(trib_ref))
    tot = summed(ones_ref)
    expg = jnp.exp(gam)
    pack_out[...] = jnp.concatenate(
        [gam, beta, expg, beta * expg, jnp.exp(tot - gam), jnp.exp(tot),
         jnp.zeros((tm, PACK_W - 6 * nh2), F32)], axis=1)

    pc = jnp.dot(hb, wcf_ref[...], preferred_element_type=F32)
    glu = pc[:, :CF_WIDTH] * _sigmoid(pc[:, CF_WIDTH:])
    _store_segments(pad_ref, glu, seg, gap)

    ps = jnp.dot(hb, wsc_ref[...], preferred_element_type=F32)
    cgxh_out[...] = ps[:, :SC_WIDTH] * ps[:, SC_WIDTH:]

    for s in range(nseg):
        for r0 in range(0, seg, ROW_BLOCK):
            base = gap + s * (seg + gap) + r0
            c = jnp.concatenate([_conv_block(pad_ref, cfconv_ref, base, ROW_BLOCK, c0, LANES, c0)
                                 for c0 in range(0, CF_WIDTH, LANES)], axis=1)
            mu = jnp.mean(c, axis=-1, keepdims=True)
            cc = c - mu
            y = cc * lax.rsqrt(jnp.mean(cc * cc, axis=-1, keepdims=True) + EPS)
            y = y * lng_ref[...] + lnb_ref[...]
            hc_out[pl.ds(s * seg + r0, ROW_BLOCK), :] = _silu(y)


def _chunk_matrices(tm):
    idx = np.arange(tm)
    same = (idx[:, None] // DN_BLOCK) == (idx[None, :] // DN_BLOCK)
    lower = same & (idx[:, None] >= idx[None, :])
    upper = same & (idx[:, None] <= idx[None, :])
    as_bf16 = lambda m: jnp.asarray(m.astype(np.float32), dtype=BF16)
    return as_bf16(lower), as_bf16(upper), as_bf16(same)


def _const_spec(shape):
    nd = len(shape)
    return pl.BlockSpec(shape, lambda *_: (0,) * nd, pipeline_mode=pl.Buffered(1))


def _mixer_in(x2d, mod, p, *, seq_len, seg):
    tokens = x2d.shape[0]
    tm = TOKENS_PER_TILE
    tiles_per_mod = max(seq_len // tm, 1) if mod.shape[0] > 1 else None
    mod_map = (lambda i: (i // tiles_per_mod, 0, 0)) if tiles_per_mod else (lambda i: (0, 0, 0))
    trif, trib, ones = _chunk_matrices(tm)
    pad_rows = CONV_GAP + (tm // seg) * (seg + CONV_GAP)
    row = lambda i: (i, 0)
    consts = [p["g_pre_mix"], p["w_qkv"], p["w_dngate"], p["w_cf"], p["w_sc"], p["dn_conv"], p["cf_conv"],
              p["cf_ln_g"], p["cf_ln_b"], p["dn_a_log"], p["dn_dt_bias"], trif, trib, ones]
    return pl.pallas_call(
        functools.partial(_mixer_in_kernel, seg=seg),
        out_shape=(jax.ShapeDtypeStruct((tokens, 2 * DN_QK + DN_V), F32),
                   jax.ShapeDtypeStruct((tokens, PACK_W), F32),
                   jax.ShapeDtypeStruct((tokens, CF_WIDTH), F32),
                   jax.ShapeDtypeStruct((tokens, SC_WIDTH), F32)),
        grid=(tokens // tm,),
        in_specs=[pl.BlockSpec((tm, D_MODEL), row), pl.BlockSpec((1, 6, D_MODEL), mod_map)]
                 + [_const_spec(c.shape) for c in consts],
        out_specs=(pl.BlockSpec((tm, 2 * DN_QK + DN_V), row), pl.BlockSpec((tm, PACK_W), row),
                   pl.BlockSpec((tm, CF_WIDTH), row), pl.BlockSpec((tm, SC_WIDTH), row)),
        scratch_shapes=[pltpu.VMEM((pad_rows, DN_QK), F32)],
        compiler_params=pltpu.CompilerParams(dimension_semantics=("parallel",),
                                             vmem_limit_bytes=VMEM_LIMIT_BYTES),
        name="mixer_in",
    )(x2d, mod, *consts)


def _substitution_masks():
    n = DN_BLOCK
    i = np.arange(n)[:, None]
    j = np.arange(n)[None, :]
    out = []
    for strict in (i > j, i < j):
        levels = [strict & (i // 2 == j // 2)]
        size = 2
        while size < n:
            levels.append(strict & (i // (2 * size) == j // (2 * size)) & (i // size != j // size))
            size *= 2
        out.append(np.stack(levels))
    return jnp.asarray(np.stack(out).astype(np.float32), dtype=BF16)


def _deltanet_kernel(qf_ref, qb_ref, pf_ref, pb_ref, masks_ref, *rest, has_s0, nbatch, same_tokens):
    if has_s0:
        s0_ref, of_ref, ob_ref, sout_ref, state_ref = rest
    else:
        of_ref, ob_ref, sout_ref, state_ref = rest
    n = pl.program_id(1)
    last = pl.num_programs(1) - 1
    c = DN_BLOCK
    nh2 = 2 * DN_HEADS
    n_levels = masks_ref.shape[1]

    @pl.when(n == 0)
    def _():
        if has_s0:
            state_ref[...] = s0_ref[...]
        else:
            state_ref[...] = jnp.zeros(state_ref.shape, F32)

    ii = lax.broadcasted_iota(jnp.int32, (c, c), 0)
    jj = lax.broadcasted_iota(jnp.int32, (c, c), 1)
    eye = (ii == jj).astype(F32).astype(BF16)
    incl = ((ii >= jj), (ii <= jj))
    strict = ((ii > jj), (ii < jj))
    q_refs, p_refs, o_refs = (qf_ref, qb_ref), (pf_ref, pb_ref), (of_ref, ob_ref)

    nblk = DN_STEP_TOKENS // c
    chains = [(bi, d, hd) for bi in range(nbatch) for d in (0, 1) for hd in range(DN_HEADS)]
    probs = [(bi, d, blk, hd) for bi, d, hd in chains for blk in range(nblk)]
    rows = lambda blk: pl.ds(blk * c, c)
    pk = {(bi, d, blk): p_refs[d][bi, rows(blk), :]
          for bi in range(nbatch) for d in (0, 1) for blk in range(nblk)}
    pk_t = {key: val.T for key, val in pk.items()}

    def col(p, grp):
        bi, d, blk, hd = p
        j = grp * nh2 + d * DN_HEADS + hd
        return pk[(bi, d, blk)][:, j:j + 1]

    def head(p, off):
        bi, d, blk, hd = p
        return q_refs[d][bi, rows(blk), pl.ds(off + hd * DN_DK, DN_DK)]

    q = {p: head(p, 0) for p in probs}
    k = {p: head(p, DN_QK) for p in probs}
    v = {p: head(p, 2 * DN_QK) for p in probs}
    kb = {p: k[p].astype(BF16) for p in probs}
    kq = {}
    for p in probs:
        bi, d, blk, hd = p
        twin = (bi, 0, blk, hd)
        if same_tokens and d == 1:
            kq[p] = kq[twin]
        else:
            kq[p] = _mm_nt(jnp.concatenate([kb[p], q[p].astype(BF16)], axis=0), kb[p])
    kk = {p: kq[p][:c] for p in probs}
    e = {}
    for p in probs:
        bi, d, blk, hd = p
        gam_r = pk_t[(bi, d, blk)][d * DN_HEADS + hd:d * DN_HEADS + hd + 1, :]
        e[p] = jnp.exp(jnp.where(incl[d], col(p, 0) - gam_r, 0.0))
    a = {p: jnp.where(strict[p[1]], col(p, 1) * kk[p] * e[p], 0.0).astype(BF16) for p in probs}
    inv = {p: eye - a[p] * masks_ref[p[1], 0] for p in probs}
    for lvl in range(1, n_levels):
        size = 2 ** lvl
        off = {p: a[p] * masks_ref[p[1], lvl] for p in probs}
        if size % BF16_SUBLANES:
            t = {p: jnp.dot(inv[p], off[p], preferred_element_type=F32).astype(BF16) for p in probs}
            inv = {p: inv[p] - jnp.dot(t[p], inv[p], preferred_element_type=F32).astype(BF16) for p in probs}
            continue
        halves = {p: inv[p].reshape(c // (2 * size), 2, size, c) for p in probs}
        moving = {p: halves[p][:, 1 - p[1]].reshape(c // 2, c) for p in probs}
        t = {p: jnp.dot(moving[p], off[p], preferred_element_type=F32).astype(BF16) for p in probs}
        moved = {p: (moving[p] - jnp.dot(t[p], inv[p], preferred_element_type=F32).astype(BF16)
                     ).reshape(c // (2 * size), 1, size, c) for p in probs}
        inv = {p: jnp.concatenate([halves[p][:, 0:1], moved[p]] if p[1] == 0 else [moved[p], halves[p][:, 1:2]],
                                  axis=1).reshape(c, c) for p in probs}
    rhs = {p: jnp.concatenate([col(p, 1) * v[p], col(p, 3) * k[p]], axis=1).astype(BF16) for p in probs}
    sol = {p: jnp.dot(inv[p], rhs[p], preferred_element_type=F32) for p in probs}
    qkd = {p: (kq[p][c:] * jnp.where(incl[p[1]], e[p], 0.0)).astype(BF16) for p in probs}
    with_state = {p: jnp.concatenate([sol[p][:, DN_DV:].astype(BF16), (q[p] * col(p, 2)).astype(BF16)], axis=0)
                  for p in probs}
    with_v_new = {p: jnp.concatenate([qkd[p], (k[p] * col(p, 4)).T.astype(BF16)], axis=0) for p in probs}

    state = {ch: state_ref[ch] for ch in chains}
    for step in range(nblk):
        cur = {(bi, d, hd): (bi, d, step if d == 0 else nblk - 1 - step, hd) for bi, d, hd in chains}
        from_state = {ch: jnp.dot(with_state[cur[ch]], state[ch].astype(BF16), preferred_element_type=F32)
                      for ch in chains}
        v_new = {ch: (sol[cur[ch]][:, :DN_DV] - from_state[ch][:c]).astype(BF16) for ch in chains}
        from_v_new = {ch: jnp.dot(with_v_new[cur[ch]], v_new[ch], preferred_element_type=F32)
                      for ch in chains}
        new_state = {}
        for ch in chains:
            bi, d, blk, hd = cur[ch]
            o_refs[d][bi, rows(blk), pl.ds(hd * DN_DV, DN_DV)] = from_state[ch][c:] + from_v_new[ch][:c]
            j = 5 * nh2 + d * DN_HEADS + hd
            etot = pk[(bi, d, blk)][0:1, j:j + 1]
            new_state[ch] = state[ch] * etot + from_v_new[ch][c:]
        state = new_state
    for ch in chains:
        state_ref[ch] = state[ch]

    @pl.when(n == last)
    def _():
        sout_ref[...] = state_ref[...]


def _deltanet(qkv, pack, s0, *, batch, seq_len):
    tokens = qkv.shape[0]
    step = DN_STEP_TOKENS
    nb = seq_len // step
    nbatch = DN_BATCH_PER_STEP
    width = 2 * DN_QK + DN_V
    fwd = lambda b, n: (b, n, 0)
    bwd = lambda b, n: (b, nb - 1 - n, 0)
    state_block = (nbatch, 2, DN_HEADS, DN_DK, DN_DV)
    state_map = lambda b, n: (b, 0, 0, 0, 0)
    masks = _substitution_masks()
    qkv3 = qkv.reshape(batch, seq_len, width)
    pack3 = pack.reshape(batch, seq_len, PACK_W)
    in_specs = [pl.BlockSpec((nbatch, step, width), fwd), pl.BlockSpec((nbatch, step, width), bwd),
                pl.BlockSpec((nbatch, step, PACK_W), fwd), pl.BlockSpec((nbatch, step, PACK_W), bwd),
                _const_spec(masks.shape)]
    args = [qkv3, qkv3, pack3, pack3, masks]
    if s0 is not None:
        in_specs.append(pl.BlockSpec(state_block, state_map))
        args.append(s0)
    o_f, o_b, states = pl.pallas_call(
        functools.partial(_deltanet_kernel, has_s0=s0 is not None, nbatch=nbatch, same_tokens=nb == 1),
        out_shape=(jax.ShapeDtypeStruct((batch, seq_len, DN_V), F32),
                   jax.ShapeDtypeStruct((batch, seq_len, DN_V), F32),
                   jax.ShapeDtypeStruct((batch, 2, DN_HEADS, DN_DK, DN_DV), F32)),
        grid=(batch // nbatch, nb),
        in_specs=in_specs,
        out_specs=(pl.BlockSpec((nbatch, step, DN_V), fwd), pl.BlockSpec((nbatch, step, DN_V), bwd),
                   pl.BlockSpec(state_block, state_map)),
        scratch_shapes=[pltpu.VMEM(state_block, F32)],
        compiler_params=pltpu.CompilerParams(dimension_semantics=("parallel", "arbitrary"),
                                             vmem_limit_bytes=VMEM_LIMIT_BYTES),
        name="deltanet",
    )(*args)
    return o_f.reshape(tokens, DN_V), o_b.reshape(tokens, DN_V), states


def _mixer_out_kernel(x_ref, of_ref, ob_ref, hc_ref, cg_ref, mod_ref, gpre_ref, gpost_ref, dnn_ref,
                      wz_ref, wbg_ref, wmg_ref, wdn_ref, wcf_ref, wsc_ref, wo_ref, scconv_ref,
                      out_ref, min_ref, *, seg, stride):
    tm = TOKENS_PER_TILE
    x = x_ref[...].reshape(tm, D_MODEL)
    hb = _modulated_rmsnorm(x, gpre_ref[...], mod_ref[0, 0:1, :], mod_ref[0, 1:2, :]).astype(BF16)

    o = of_ref[...].reshape(tm, DN_V) + ob_ref[...].reshape(tm, DN_V)
    z = jnp.dot(hb, wz_ref[...], preferred_element_type=F32)
    heads = []
    for hd in range(DN_HEADS):
        oh = _rmsnorm(o[:, hd * DN_DV:(hd + 1) * DN_DV], dnn_ref[...])
        heads.append((oh * _silu(z[:, hd * DN_DV:(hd + 1) * DN_DV])).astype(BF16))
    o_gated = jnp.concatenate(heads, axis=1)

    cg_pad = _pad_rows(cg_ref[...].reshape(tm, SC_WIDTH))
    bg = jnp.dot(hb, wbg_ref[...], preferred_element_type=F32)
    w_sc = scconv_ref[...]
    sc_parts = [_conv3_block(cg_pad, w_sc, seg, stride, r0, ROW_BLOCK) * bg[r0:r0 + ROW_BLOCK]
                for r0 in range(0, tm, ROW_BLOCK)]
    sc_in = jnp.concatenate(sc_parts, axis=0).astype(BF16)
    hc = hc_ref[...].reshape(tm, CF_WIDTH).astype(BF16)

    cb = 256
    for c0 in range(0, D_MODEL, cb):
        cols = slice(c0, c0 + cb)
        y_dn = jnp.dot(o_gated, wdn_ref[:, cols], preferred_element_type=F32)
        y_cf = jnp.dot(hc, wcf_ref[:, cols], preferred_element_type=F32)
        y_sc = jnp.dot(sc_in, wsc_ref[:, cols], preferred_element_type=F32)
        ga = _sigmoid(jnp.dot(hb, wmg_ref[:, c0:c0 + cb], preferred_element_type=F32))
        gb = _sigmoid(jnp.dot(hb, wmg_ref[:, D_MODEL + c0:D_MODEL + c0 + cb], preferred_element_type=F32))
        gc = _sigmoid(jnp.dot(hb, wmg_ref[:, 2 * D_MODEL + c0:2 * D_MODEL + c0 + cb], preferred_element_type=F32))
        min_ref[:, cols] = (ga * y_dn + gb * y_cf + gc * y_sc).astype(BF16)

    m = jnp.dot(min_ref[...], wo_ref[...], preferred_element_type=F32)
    out = x + mod_ref[0, 2:3, :] * _rmsnorm(m, gpost_ref[...])
    out_ref[...] = out.reshape(out_ref.shape)


def _ffn_kernel(x_ref, mod_ref, gpre_ref, gpost_ref, wup_ref, wdown_ref, conv_ref,
                out_ref, act_ref, *, seg, stride):
    tm = TOKENS_PER_TILE
    x = x_ref[...].reshape(tm, D_MODEL)
    hb = _modulated_rmsnorm(x, gpre_ref[...], mod_ref[0, 3:4, :], mod_ref[0, 4:5, :]).astype(BF16)
    for f0 in range(0, D_FF, FF_BLOCK):
        ua = _pad_rows(jnp.dot(hb, wup_ref[:, f0:f0 + FF_BLOCK], preferred_element_type=F32))
        ub = _pad_rows(jnp.dot(hb, wup_ref[:, D_FF + f0:D_FF + f0 + FF_BLOCK], preferred_element_type=F32))
        wa = conv_ref[:, f0:f0 + FF_BLOCK]
        wb = conv_ref[:, D_FF + f0:D_FF + f0 + FF_BLOCK]
        for r0 in range(0, tm, ROW_BLOCK):
            ca = _conv3_block(ua, wa, seg, stride, r0, ROW_BLOCK)
            cb = _conv3_block(ub, wb, seg, stride, r0, ROW_BLOCK)
            act_ref[pl.ds(r0, ROW_BLOCK), pl.ds(f0, FF_BLOCK)] = (_silu(ca) * cb).astype(BF16)
    y = jnp.dot(act_ref[...], wdown_ref[...], preferred_element_type=F32)
    out = x + mod_ref[0, 5:6, :] * _rmsnorm(y, gpost_ref[...])
    out_ref[...] = out.reshape(out_ref.shape)


def _tile_view(a2d, on_grid, batch):
    width = a2d.shape[1]
    if on_grid:
        return a2d.reshape(batch, a2d.shape[0] // (batch * GRID_W), GRID_W, width)
    return a2d.reshape(a2d.shape[0] // TOKENS_PER_TILE, TOKENS_PER_TILE // COL_TILE_W, COL_TILE_W, width)


def _tile_spec(width, on_grid):
    block = (1, TOKENS_PER_TILE // COL_TILE_W, COL_TILE_W, width)
    if on_grid:
        per_batch = GRID_W // COL_TILE_W
        return pl.BlockSpec(block, lambda i: (i // per_batch, 0, i % per_batch, 0))
    return pl.BlockSpec(block, lambda i: (i, 0, 0, 0))


def _mod_spec(mod, on_grid):
    if on_grid:
        per_batch = GRID_W // COL_TILE_W
        return pl.BlockSpec((1, 6, D_MODEL), lambda i: (i // per_batch, 0, 0))
    return pl.BlockSpec((1, 6, D_MODEL), lambda i: (0, 0, 0))


def _v_conv_geometry(on_grid):
    return (TOKENS_PER_TILE, COL_TILE_W) if on_grid else (256, 1)


def _mixer_out(x2d, o_f, o_b, hc, cgxh, mod, p, *, on_grid, batch):
    tokens = x2d.shape[0]
    seg, stride = _v_conv_geometry(on_grid)
    consts = [p["g_pre_mix"], p["g_post_mix"], p["dn_norm_g"], p["w_z"], p["w_bg"], p["w_mgate"],
              p["w_dn_out"], p["w_cf_out"], p["w_sc_out"], p["w_o"], p["sc_conv"]]
    tiled = [x2d, o_f, o_b, hc, cgxh]
    out = pl.pallas_call(
        functools.partial(_mixer_out_kernel, seg=seg, stride=stride),
        out_shape=jax.ShapeDtypeStruct(_tile_view(x2d, on_grid, batch).shape, F32),
        grid=(tokens // TOKENS_PER_TILE,),
        in_specs=[_tile_spec(a.shape[1], on_grid) for a in tiled] + [_mod_spec(mod, on_grid)]
                 + [_const_spec(c.shape) for c in consts],
        out_specs=_tile_spec(D_MODEL, on_grid),
        scratch_shapes=[pltpu.VMEM((TOKENS_PER_TILE, D_MODEL), BF16)],
        compiler_params=pltpu.CompilerParams(dimension_semantics=("parallel",),
                                             vmem_limit_bytes=VMEM_LIMIT_BYTES),
        name="mixer_out",
    )(*[_tile_view(a, on_grid, batch) for a in tiled], mod, *consts)
    return out.reshape(tokens, D_MODEL)


def _ffn(x2d, mod, p, *, on_grid, batch):
    tokens = x2d.shape[0]
    seg, stride = _v_conv_geometry(on_grid)
    consts = [p["g_pre_ffn"], p["g_post_ffn"], p["w_ffn_up"], p["w_ffn_down"], p["ffn_conv"]]
    out = pl.pallas_call(
        functools.partial(_ffn_kernel, seg=seg, stride=stride),
        out_shape=jax.ShapeDtypeStruct(_tile_view(x2d, on_grid, batch).shape, F32),
        grid=(tokens // TOKENS_PER_TILE,),
        in_specs=[_tile_spec(D_MODEL, on_grid), _mod_spec(mod, on_grid)] + [_const_spec(c.shape) for c in consts],
        out_specs=_tile_spec(D_MODEL, on_grid),
        scratch_shapes=[pltpu.VMEM((TOKENS_PER_TILE, D_FF), BF16)],
        compiler_params=pltpu.CompilerParams(dimension_semantics=("parallel",),
                                             vmem_limit_bytes=VMEM_LIMIT_BYTES),
        name="ffn",
    )(_tile_view(x2d, on_grid, batch), mod, *consts)
    return out.reshape(tokens, D_MODEL)


def _layer_params(l, g_pre_mix, g_post_mix, g_pre_ffn, g_post_ffn, w_in, dn_conv, dn_a_log, dn_dt_bias,
                  dn_norm_g, w_dn_out, cf_conv, cf_ln_g, cf_ln_b, w_cf_out, sc_conv, w_sc_out, w_o,
                  w_ffn_up, ffn_conv, w_ffn_down):
    w = lambda lo, hi: w_in[l, :, lo:hi].astype(BF16)
    row = lambda v: v.reshape(1, -1).astype(F32)
    return {
        "g_pre_mix": row(g_pre_mix[l]), "g_post_mix": row(g_post_mix[l]),
        "g_pre_ffn": row(g_pre_ffn[l]), "g_post_ffn": row(g_post_ffn[l]),
        "w_qkv": w(_OFF_Q, _OFF_Z), "w_z": w(_OFF_Z, _OFF_A), "w_dngate": w(_OFF_A, _OFF_CF),
        "w_cf": w(_OFF_CF, _OFF_SC), "w_bg": w(_OFF_SC, _OFF_SC + SC_WIDTH),
        "w_sc": w(_OFF_SC + SC_WIDTH, _OFF_GATE), "w_mgate": w(_OFF_GATE, _OFF_GATE + 3 * D_MODEL),
        "dn_conv": dn_conv[l], "dn_a_log": row(dn_a_log[l]), "dn_dt_bias": row(dn_dt_bias[l]),
        "dn_norm_g": row(dn_norm_g[l]), "w_dn_out": w_dn_out[l].astype(BF16),
        "cf_conv": cf_conv[l], "cf_ln_g": row(cf_ln_g[l]), "cf_ln_b": row(cf_ln_b[l]),
        "w_cf_out": w_cf_out[l].astype(BF16), "sc_conv": sc_conv[l], "w_sc_out": w_sc_out[l].astype(BF16),
        "w_o": w_o[l].astype(BF16), "w_ffn_up": w_ffn_up[l].astype(BF16), "ffn_conv": ffn_conv[l],
        "w_ffn_down": w_ffn_down[l].astype(BF16),
    }


def _trunk_layer(x2d, mod, p, s0, *, batch, seq_len, on_grid):
    seg_h = GRID_W if on_grid else seq_len
    qkv, pack, hc, cgxh = _mixer_in(x2d, mod, p, seq_len=seq_len, seg=seg_h)
    o_f, o_b, states = _deltanet(qkv, pack, s0, batch=batch, seq_len=seq_len)
    x2d = _mixer_out(x2d, o_f, o_b, hc, cgxh, mod, p, on_grid=on_grid, batch=batch)
    x2d = _ffn(x2d, mod, p, on_grid=on_grid, batch=batch)
    return x2d, states


def kernel(x_prompt, x_sample, state_dn, c, c_ctx, w_mod, b_mod, g_pre_mix, g_post_mix, g_pre_ffn, g_post_ffn,
           w_in, dn_conv, dn_a_log, dn_dt_bias, dn_norm_g, w_dn_out, cf_conv, cf_ln_g, cf_ln_b, w_cf_out,
           sc_conv, w_sc_out, w_o, w_ffn_up, ffn_conv, w_ffn_down):
    batch, seq, _ = x_prompt.shape
    dec_batch, dec_seq, _ = x_sample.shape
    assert dec_batch + 1 <= 8
    c_all = jnp.concatenate([c, c_ctx[None, :], jnp.zeros((8 - dec_batch - 1, D_MODEL), F32)], axis=0)
    mods = _modulation(c_all, w_mod, b_mod)
    xp = x_prompt.reshape(batch * seq, D_MODEL)
    xs = x_sample.reshape(dec_batch * dec_seq, D_MODEL)
    ctx_states = []
    for l in range(DEPTH):
        p = _layer_params(l, g_pre_mix, g_post_mix, g_pre_ffn, g_post_ffn, w_in, dn_conv, dn_a_log, dn_dt_bias,
                          dn_norm_g, w_dn_out, cf_conv, cf_ln_g, cf_ln_b, w_cf_out, sc_conv, w_sc_out, w_o,
                          w_ffn_up, ffn_conv, w_ffn_down)
        mod_lat = mods[l, :dec_batch].reshape(dec_batch, 6, D_MODEL)
        mod_ctx = mods[l, dec_batch:dec_batch + 1].reshape(1, 6, D_MODEL)
        xp, st = _trunk_layer(xp, mod_ctx, p, None, batch=batch, seq_len=seq, on_grid=False)
        ctx_states.append(st)
        xs, _ = _trunk_layer(xs, mod_lat, p, state_dn[:, l].astype(F32), batch=dec_batch, seq_len=dec_seq,
                             on_grid=True)
    new_state = jnp.stack(ctx_states, axis=1).astype(x_prompt.dtype)
    return (xp.reshape(batch, seq, D_MODEL), xs.reshape(dec_batch, dec_seq, D_MODEL), new_state)
```

```python
import functools

import numpy as np
import jax
import jax.numpy as jnp
from jax import lax
from jax.experimental import pallas as pl
from jax.experimental.pallas import tpu as pltpu

F32 = jnp.float32
BF16 = jnp.bfloat16

D_MODEL = 1024
DEPTH = 2
GRID_W = 64
DN_HEADS = 4
DN_DK = 128
DN_DV = 128
DN_BLOCK = 128
DN_STEP_TOKENS = 256
DN_BATCH_PER_STEP = 2
DN_QK = DN_HEADS * DN_DK
DN_V = DN_HEADS * DN_DV
CF_WIDTH = 512
CF_CONV_W = 31
SC_WIDTH = 512
D_FF = 2816
EPS = 1e-6

_OFF_Q = 0
_OFF_Z = 2 * DN_QK + DN_V
_OFF_A = _OFF_Z + DN_V
_OFF_CF = _OFF_A + 4 * DN_HEADS
_OFF_SC = _OFF_CF + 2 * CF_WIDTH
_OFF_GATE = _OFF_SC + 3 * SC_WIDTH

SUBLANES = 8
BF16_SUBLANES = 16
LANES = 128
TOKENS_PER_TILE = 512
COL_TILE_W = 8
CONV_GAP = 16
ROW_BLOCK = 64
FF_BLOCK = 256
PACK_W = 128
VMEM_LIMIT_BYTES = 56 * 1024 * 1024


def _sigmoid(x):
    return 0.5 * jnp.tanh(0.5 * x) + 0.5


def _silu(x):
    h = 0.5 * x
    return h + h * jnp.tanh(h)


def _softplus(x):
    return jnp.maximum(x, 0.0) + jnp.log1p(jnp.exp(-jnp.abs(x)))


def _mm(a, b):
    return jnp.dot(a.astype(BF16), b.astype(BF16), preferred_element_type=F32)


def _mm_nt(a, b):
    return lax.dot_general(a.astype(BF16), b.astype(BF16), (((1,), (1,)), ((), ())),
                           preferred_element_type=F32)


def _modulated_rmsnorm(x, gain, shift, scale):
    y = x * lax.rsqrt(jnp.mean(x * x, axis=-1, keepdims=True) + EPS)
    return y * (gain * (1.0 + scale)) + shift


def _rmsnorm(x, gain):
    return x * lax.rsqrt(jnp.mean(x * x, axis=-1, keepdims=True) + EPS) * gain


def _store_segments(pad_ref, val, seg, gap):
    rows, width = val.shape
    zeros = jnp.zeros((gap, width), F32)
    pad_ref[pl.ds(0, gap), pl.ds(0, width)] = zeros
    for s in range(rows // seg):
        base = gap + s * (seg + gap)
        pad_ref[pl.ds(base, seg), pl.ds(0, width)] = val[s * seg:(s + 1) * seg]
        pad_ref[pl.ds(base + seg, gap), pl.ds(0, width)] = zeros


def _conv_block(pad_ref, w_ref, base, rows, c0, cols, w_c0):
    taps = w_ref.shape[0]
    offs = [k - taps // 2 for k in range(taps)]
    halo = SUBLANES * (-(-max(offs) // SUBLANES))
    n = rows + 2 * halo
    win = pad_ref[pl.ds(base - halo, n), pl.ds(c0, cols)]
    by_residue = {}
    for k, off in enumerate(offs):
        by_residue.setdefault(off % SUBLANES, []).append((k, off // SUBLANES))
    acc = None
    for b in sorted(by_residue):
        rolled = win if b == 0 else pltpu.roll(win, n - b, axis=0)
        for k, a in by_residue[b]:
            start = halo + SUBLANES * a
            term = rolled[start:start + rows] * w_ref[pl.ds(k, 1), pl.ds(w_c0, cols)]
            acc = term if acc is None else acc + term
    return acc


def _pad_rows(u):
    zeros = jnp.zeros((SUBLANES, u.shape[1]), u.dtype)
    return jnp.concatenate([zeros, u, zeros], axis=0)


def _conv3_block(u_pad, w, seg, stride, r0, rows):
    cur = u_pad[r0 + SUBLANES:r0 + SUBLANES + rows]
    if stride == SUBLANES:
        prev = u_pad[r0:r0 + rows]
        nxt = u_pad[r0 + 2 * SUBLANES:r0 + 2 * SUBLANES + rows]
    else:
        assert stride == 1 and seg % rows == 0
        n = rows + 2 * SUBLANES
        win = u_pad[r0:r0 + n]
        prev = pltpu.roll(win, 1, axis=0)[SUBLANES:SUBLANES + rows]
        nxt = pltpu.roll(win, n - 1, axis=0)[SUBLANES:SUBLANES + rows]
        sub = lax.broadcasted_iota(jnp.int32, (SUBLANES, u_pad.shape[1]), 0)
        if r0 % seg == 0:
            prev = jnp.concatenate([jnp.where(sub == 0, 0.0, prev[:SUBLANES]), prev[SUBLANES:]], axis=0)
        if (r0 + rows) % seg == 0:
            nxt = jnp.concatenate([nxt[:-SUBLANES], jnp.where(sub == SUBLANES - 1, 0.0, nxt[-SUBLANES:])], axis=0)
    return prev * w[0:1] + cur * w[1:2] + nxt * w[2:3]


def _mod_kernel(c_ref, w_ref, b_ref, o_ref):
    s = _silu(c_ref[...])
    o_ref[0] = _mm(s, w_ref[0]) + b_ref[0]


def _modulation(c_all, w_mod, b_mod):
    tn = 1024
    n_cols = 6 * D_MODEL
    return pl.pallas_call(
        _mod_kernel,
        out_shape=jax.ShapeDtypeStruct((DEPTH, 8, n_cols), F32),
        grid=(DEPTH, n_cols // tn),
        in_specs=[
            pl.BlockSpec((8, D_MODEL), lambda l, j: (0, 0)),
            pl.BlockSpec((1, D_MODEL, tn), lambda l, j: (l, 0, j)),
            pl.BlockSpec((1, 1, tn), lambda l, j: (l, 0, j)),
        ],
        out_specs=pl.BlockSpec((1, 8, tn), lambda l, j: (l, 0, j)),
        compiler_params=pltpu.CompilerParams(dimension_semantics=("parallel", "parallel"),
                                             vmem_limit_bytes=VMEM_LIMIT_BYTES),
        name="modulation",
    )(c_all, w_mod, b_mod.reshape(DEPTH, 1, n_cols))


def _mixer_in_kernel(x_ref, mod_ref, gpre_ref, wqkv_ref, wgate_ref, wcf_ref, wsc_ref,
                     dnconv_ref, cfconv_ref, lng_ref, lnb_ref, alog_ref, dtb_ref,
                     trif_ref, trib_ref, ones_ref,
                     qkv_out, pack_out, hc_out, cgxh_out,
                     pad_ref, *, seg):
    tm = x_ref.shape[0]
    nseg = tm // seg
    gap = CONV_GAP
    hb = _modulated_rmsnorm(x_ref[...], gpre_ref[...], mod_ref[0, 0:1, :], mod_ref[0, 1:2, :]).astype(BF16)

    for grp in range(3):
        proj = jnp.dot(hb, wqkv_ref[:, grp * DN_QK:(grp + 1) * DN_QK], preferred_element_type=F32)
        _store_segments(pad_ref, proj, seg, gap)
        for s in range(nseg):
            for r0 in range(0, seg, ROW_BLOCK):
                base = gap + s * (seg + gap) + r0
                rows = pl.ds(s * seg + r0, ROW_BLOCK)
                for hd in range(DN_HEADS):
                    col = grp * DN_QK + hd * DN_DK
                    a = _silu(_conv_block(pad_ref, dnconv_ref, base, ROW_BLOCK, hd * DN_DK, DN_DK, col))
                    if grp < 2:
                        inv_norm = lax.rsqrt(jnp.sum(a * a, axis=-1, keepdims=True) + EPS)
                        a = a * (inv_norm * DN_DK ** -0.5 if grp == 0 else inv_norm)
                    qkv_out[rows, pl.ds(col, DN_DK)] = a

    nh2 = 2 * DN_HEADS
    ab = jnp.dot(hb, wgate_ref[...], preferred_element_type=F32)
    g = -jnp.exp(alog_ref[...]) * _softplus(ab[:, 0:nh2] + dtb_ref[...])
    beta = _sigmoid(ab[:, nh2:2 * nh2])
    g1 = g.astype(BF16)
    r1 = g - g1.astype(F32)
    g2 = r1.astype(BF16)
    g3 = (r1 - g2.astype(F32)).astype(BF16)
    pieces = jnp.concatenate([g1, g2, g3], axis=1)

    def summed(mat_ref):
        c = jnp.dot(mat_ref[...], pieces, preferred_element_type=F32)
        return c[:, 0:nh2] + c[:, nh2:2 * nh2] + c[:, 2 * nh2:3 * nh2]

    lane = lax.broadcasted_iota(jnp.int32, (tm, nh2), 1)
    gam = jnp.where(lane < DN_HEADS, summed(trif_ref), summed(trib_ref))
    tot = summed(ones_ref)
    expg = jnp.exp(gam)
    pack_out[...] = jnp.concatenate(
        [gam, beta, expg, beta * expg, jnp.exp(tot - gam), jnp.exp(tot),
         jnp.zeros((tm, PACK_W - 6 * nh2), F32)], axis=1)

    pc = jnp.dot(hb, wcf_ref[...], preferred_element_type=F32)
    glu = pc[:, :CF_WIDTH] * _sigmoid(pc[:, CF_WIDTH:])
    _store_segments(pad_ref, glu, seg, gap)

    ps = jnp.dot(hb, wsc_ref[...], preferred_element_type=F32)
    cgxh_out[...] = ps[:, :SC_WIDTH] * ps[:, SC_WIDTH:]

    for s in range(nseg):
        for r0 in range(0, seg, ROW_BLOCK):
            base = gap + s * (seg + gap) + r0
            c = jnp.concatenate([_conv_block(pad_ref, cfconv_ref, base, ROW_BLOCK, c0, LANES, c0)
                                 for c0 in range(0, CF_WIDTH, LANES)], axis=1)
            mu = jnp.mean(c, axis=-1, keepdims=True)
            cc = c - mu
            y = cc * lax.rsqrt(jnp.mean(cc * cc, axis=-1, keepdims=True) + EPS)
            y = y * lng_ref[...] + lnb_ref[...]
            hc_out[pl.ds(s * seg + r0, ROW_BLOCK), :] = _silu(y)


def _chunk_matrices(tm):
    idx = np.arange(tm)
    same = (idx[:, None] // DN_BLOCK) == (idx[None, :] // DN_BLOCK)
    lower = same & (idx[:, None] >= idx[None, :])
    upper = same & (idx[:, None] <= idx[None, :])
    as_bf16 = lambda m: jnp.asarray(m.astype(np.float32), dtype=BF16)
    return as_bf16(lower), as_bf16(upper), as_bf16(same)


def _const_spec(shape):
    nd = len(shape)
    return pl.BlockSpec(shape, lambda *_: (0,) * nd, pipeline_mode=pl.Buffered(1))


def _mixer_in(x2d, mod, p, *, seq_len, seg):
    tokens = x2d.shape[0]
    tm = TOKENS_PER_TILE
    tiles_per_mod = max(seq_len // tm, 1) if mod.shape[0] > 1 else None
    mod_map = (lambda i: (i // tiles_per_mod, 0, 0)) if tiles_per_mod else (lambda i: (0, 0, 0))
    trif, trib, ones = _chunk_matrices(tm)
    pad_rows = CONV_GAP + (tm // seg) * (seg + CONV_GAP)
    row = lambda i: (i, 0)
    consts = [p["g_pre_mix"], p["w_qkv"], p["w_dngate"], p["w_cf"], p["w_sc"], p["dn_conv"], p["cf_conv"],
              p["cf_ln_g"], p["cf_ln_b"], p["dn_a_log"], p["dn_dt_bias"], trif, trib, ones]
    return pl.pallas_call(
        functools.partial(_mixer_in_kernel, seg=seg),
        out_shape=(jax.ShapeDtypeStruct((tokens, 2 * DN_QK + DN_V), F32),
                   jax.ShapeDtypeStruct((tokens, PACK_W), F32),
                   jax.ShapeDtypeStruct((tokens, CF_WIDTH), F32),
                   jax.ShapeDtypeStruct((tokens, SC_WIDTH), F32)),
        grid=(tokens // tm,),
        in_specs=[pl.BlockSpec((tm, D_MODEL), row), pl.BlockSpec((1, 6, D_MODEL), mod_map)]
                 + [_const_spec(c.shape) for c in consts],
        out_specs=(pl.BlockSpec((tm, 2 * DN_QK + DN_V), row), pl.BlockSpec((tm, PACK_W), row),
                   pl.BlockSpec((tm, CF_WIDTH), row), pl.BlockSpec((tm, SC_WIDTH), row)),
        scratch_shapes=[pltpu.VMEM((pad_rows, DN_QK), F32)],
        compiler_params=pltpu.CompilerParams(dimension_semantics=("parallel",),
                                             vmem_limit_bytes=VMEM_LIMIT_BYTES),
        name="mixer_in",
    )(x2d, mod, *consts)


def _substitution_masks():
    n = DN_BLOCK
    i = np.arange(n)[:, None]
    j = np.arange(n)[None, :]
    out = []
    for strict in (i > j, i < j):
        levels = [strict & (i // 2 == j // 2)]
        size = 2
        while size < n:
            levels.append(strict & (i // (2 * size) == j // (2 * size)) & (i // size != j // size))
            size *= 2
        out.append(np.stack(levels))
    return jnp.asarray(np.stack(out).astype(np.float32), dtype=BF16)


def _deltanet_kernel(qf_ref, qb_ref, pf_ref, pb_ref, masks_ref, *rest, has_s0, nbatch, same_tokens):
    if has_s0:
        s0_ref, of_ref, ob_ref, sout_ref, state_ref = rest
    else:
        of_ref, ob_ref, sout_ref, state_ref = rest
    n = pl.program_id(1)
    last = pl.num_programs(1) - 1
    c = DN_BLOCK
    nh2 = 2 * DN_HEADS
    n_levels = masks_ref.shape[1]

    @pl.when(n == 0)
    def _():
        if has_s0:
            state_ref[...] = s0_ref[...]
        else:
            state_ref[...] = jnp.zeros(state_ref.shape, F32)

    ii = lax.broadcasted_iota(jnp.int32, (c, c), 0)
    jj = lax.broadcasted_iota(jnp.int32, (c, c), 1)
    eye = (ii == jj).astype(F32).astype(BF16)
    incl = ((ii >= jj), (ii <= jj))
    strict = ((ii > jj), (ii < jj))
    q_refs, p_refs, o_refs = (qf_ref, qb_ref), (pf_ref, pb_ref), (of_ref, ob_ref)

    nblk = DN_STEP_TOKENS // c
    chains = [(bi, d, hd) for bi in range(nbatch) for d in (0, 1) for hd in range(DN_HEADS)]
    probs = [(bi, d, blk, hd) for bi, d, hd in chains for blk in range(nblk)]
    rows = lambda blk: pl.ds(blk * c, c)
    pk = {(bi, d, blk): p_refs[d][bi, rows(blk), :]
          for bi in range(nbatch) for d in (0, 1) for blk in range(nblk)}
    pk_t = {key: val.T for key, val in pk.items()}

    def col(p, grp):
        bi, d, blk, hd = p
        j = grp * nh2 + d * DN_HEADS + hd
        return pk[(bi, d, blk)][:, j:j + 1]

    def head(p, off):
        bi, d, blk, hd = p
        return q_refs[d][bi, rows(blk), pl.ds(off + hd * DN_DK, DN_DK)]

    q = {p: head(p, 0) for p in probs}
    k = {p: head(p, DN_QK) for p in probs}
    v = {p: head(p, 2 * DN_QK) for p in probs}
    kb = {p: k[p].astype(BF16) for p in probs}
    kq = {}
    for p in probs:
        bi, d, blk, hd = p
        twin = (bi, 0, blk, hd)
        if same_tokens and d == 1:
            kq[p] = kq[twin]
        else:
            kq[p] = _mm_nt(jnp.concatenate([kb[p], q[p].astype(BF16)], axis=0), kb[p])
    kk = {p: kq[p][:c] for p in probs}
    e = {}
    for p in probs:
        bi, d, blk, hd = p
        gam_r = pk_t[(bi, d, blk)][d * DN_HEADS + hd:d * DN_HEADS + hd + 1, :]
        e[p] = jnp.exp(jnp.where(incl[d], col(p, 0) - gam_r, 0.0))
    a = {p: jnp.where(strict[p[1]], col(p, 1) * kk[p] * e[p], 0.0).astype(BF16) for p in probs}
    inv = {p: eye - a[p] * masks_ref[p[1], 0] for p in probs}
    for lvl in range(1, n_levels):
        size = 2 ** lvl
        off = {p: a[p] * masks_ref[p[1], lvl] for p in probs}
        if size % BF16_SUBLANES:
            t = {p: jnp.dot(inv[p], off[p], preferred_element_type=F32).astype(BF16) for p in probs}
            inv = {p: inv[p] - jnp.dot(t[p], inv[p], preferred_element_type=F32).astype(BF16) for p in probs}
            continue
        halves = {p: inv[p].reshape(c // (2 * size), 2, size, c) for p in probs}
        moving = {p: halves[p][:, 1 - p[1]].reshape(c // 2, c) for p in probs}
        t = {p: jnp.dot(moving[p], off[p], preferred_element_type=F32).astype(BF16) for p in probs}
        moved = {p: (moving[p] - jnp.dot(t[p], inv[p], preferred_element_type=F32).astype(BF16)
                     ).reshape(c // (2 * size), 1, size, c) for p in probs}
        inv = {p: jnp.concatenate([halves[p][:, 0:1], moved[p]] if p[1] == 0 else [moved[p], halves[p][:, 1:2]],
                                  axis=1).reshape(c, c) for p in probs}
    rhs = {p: jnp.concatenate([col(p, 1) * v[p], col(p, 3) * k[p]], axis=1).astype(BF16) for p in probs}
    sol = {p: jnp.dot(inv[p], rhs[p], preferred_element_type=F32) for p in probs}
    qkd = {p: (kq[p][c:] * jnp.where(incl[p[1]], e[p], 0.0)).astype(BF16) for p in probs}
    with_state = {p: jnp.concatenate([sol[p][:, DN_DV:].astype(BF16), (q[p] * col(p, 2)).astype(BF16)], axis=0)
                  for p in probs}
    with_v_new = {p: jnp.concatenate([qkd[p], (k[p] * col(p, 4)).T.astype(BF16)], axis=0) for p in probs}

    state = {ch: state_ref[ch] for ch in chains}
    for step in range(nblk):
        cur = {(bi, d, hd): (bi, d, step if d == 0 else nblk - 1 - step, hd) for bi, d, hd in chains}
        from_state = {ch: jnp.dot(with_state[cur[ch]], state[ch].astype(BF16), preferred_element_type=F32)
                      for ch in chains}
        v_new = {ch: (sol[cur[ch]][:, :DN_DV] - from_state[ch][:c]).astype(BF16) for ch in chains}
        from_v_new = {ch: jnp.dot(with_v_new[cur[ch]], v_new[ch], preferred_element_type=F32)
                      for ch in chains}
        new_state = {}
        for ch in chains:
            bi, d, blk, hd = cur[ch]
            o_refs[d][bi, rows(blk), pl.ds(hd * DN_DV, DN_DV)] = from_state[ch][c:] + from_v_new[ch][:c]
            j = 5 * nh2 + d * DN_HEADS + hd
            etot = pk[(bi, d, blk)][0:1, j:j + 1]
            new_state[ch] = state[ch] * etot + from_v_new[ch][c:]
        state = new_state
    for ch in chains:
        state_ref[ch] = state[ch]

    @pl.when(n == last)
    def _():
        sout_ref[...] = state_ref[...]


def _deltanet(qkv, pack, s0, *, batch, seq_len):
    tokens = qkv.shape[0]
    step = DN_STEP_TOKENS
    nb = seq_len // step
    nbatch = DN_BATCH_PER_STEP if batch % DN_BATCH_PER_STEP == 0 else 1
    width = 2 * DN_QK + DN_V
    fwd = lambda b, n: (b, n, 0)
    bwd = lambda b, n: (b, nb - 1 - n, 0)
    state_block = (nbatch, 2, DN_HEADS, DN_DK, DN_DV)
    state_map = lambda b, n: (b, 0, 0, 0, 0)
    masks = _substitution_masks()
    qkv3 = qkv.reshape(batch, seq_len, width)
    pack3 = pack.reshape(batch, seq_len, PACK_W)
    in_specs = [pl.BlockSpec((nbatch, step, width), fwd), pl.BlockSpec((nbatch, step, width), bwd),
                pl.BlockSpec((nbatch, step, PACK_W), fwd), pl.BlockSpec((nbatch, step, PACK_W), bwd),
                _const_spec(masks.shape)]
    args = [qkv3, qkv3, pack3, pack3, masks]
    if s0 is not None:
        in_specs.append(pl.BlockSpec(state_block, state_map))
        args.append(s0)
    o_f, o_b, states = pl.pallas_call(
        functools.partial(_deltanet_kernel, has_s0=s0 is not None, nbatch=nbatch, same_tokens=nb == 1),
        out_shape=(jax.ShapeDtypeStruct((batch, seq_len, DN_V), F32),
                   jax.ShapeDtypeStruct((batch, seq_len, DN_V), F32),
                   jax.ShapeDtypeStruct((batch, 2, DN_HEADS, DN_DK, DN_DV), F32)),
        grid=(batch // nbatch, nb),
        in_specs=in_specs,
        out_specs=(pl.BlockSpec((nbatch, step, DN_V), fwd), pl.BlockSpec((nbatch, step, DN_V), bwd),
                   pl.BlockSpec(state_block, state_map)),
        scratch_shapes=[pltpu.VMEM(state_block, F32)],
        compiler_params=pltpu.CompilerParams(dimension_semantics=("parallel", "arbitrary"),
                                             vmem_limit_bytes=VMEM_LIMIT_BYTES),
        name="deltanet",
    )(*args)
    return o_f.reshape(tokens, DN_V), o_b.reshape(tokens, DN_V), states


def _mixer_out_kernel(x_ref, of_ref, ob_ref, hc_ref, cg_ref, mod_ref, gpre_ref, gpost_ref, dnn_ref,
                      wz_ref, wbg_ref, wmg_ref, wdn_ref, wcf_ref, wsc_ref, wo_ref, scconv_ref,
                      out_ref, min_ref, *, seg, stride):
    tm = TOKENS_PER_TILE
    x = x_ref[...].reshape(tm, D_MODEL)
    hb = _modulated_rmsnorm(x, gpre_ref[...], mod_ref[0, 0:1, :], mod_ref[0, 1:2, :]).astype(BF16)

    o = of_ref[...].reshape(tm, DN_V) + ob_ref[...].reshape(tm, DN_V)
    z = jnp.dot(hb, wz_ref[...], preferred_element_type=F32)
    heads = []
    for hd in range(DN_HEADS):
        oh = _rmsnorm(o[:, hd * DN_DV:(hd + 1) * DN_DV], dnn_ref[...])
        heads.append((oh * _silu(z[:, hd * DN_DV:(hd + 1) * DN_DV])).astype(BF16))
    o_gated = jnp.concatenate(heads, axis=1)

    cg_pad = _pad_rows(cg_ref[...].reshape(tm, SC_WIDTH))
    bg = jnp.dot(hb, wbg_ref[...], preferred_element_type=F32)
    w_sc = scconv_ref[...]
    sc_parts = [_conv3_block(cg_pad, w_sc, seg, stride, r0, ROW_BLOCK) * bg[r0:r0 + ROW_BLOCK]
                for r0 in range(0, tm, ROW_BLOCK)]
    sc_in = jnp.concatenate(sc_parts, axis=0).astype(BF16)
    hc = hc_ref[...].reshape(tm, CF_WIDTH).astype(BF16)

    cb = 256
    for c0 in range(0, D_MODEL, cb):
        cols = slice(c0, c0 + cb)
        y_dn = jnp.dot(o_gated, wdn_ref[:, cols], preferred_element_type=F32)
        y_cf = jnp.dot(hc, wcf_ref[:, cols], preferred_element_type=F32)
        y_sc = jnp.dot(sc_in, wsc_ref[:, cols], preferred_element_type=F32)
        ga = _sigmoid(jnp.dot(hb, wmg_ref[:, c0:c0 + cb], preferred_element_type=F32))
        gb = _sigmoid(jnp.dot(hb, wmg_ref[:, D_MODEL + c0:D_MODEL + c0 + cb], preferred_element_type=F32))
        gc = _sigmoid(jnp.dot(hb, wmg_ref[:, 2 * D_MODEL + c0:2 * D_MODEL + c0 + cb], preferred_element_type=F32))
        min_ref[:, cols] = (ga * y_dn + gb * y_cf + gc * y_sc).astype(BF16)

    m = jnp.dot(min_ref[...], wo_ref[...], preferred_element_type=F32)
    out = x + mod_ref[0, 2:3, :] * _rmsnorm(m, gpost_ref[...])
    out_ref[...] = out.reshape(out_ref.shape)


def _ffn_kernel(x_ref, mod_ref, gpre_ref, gpost_ref, wup_ref, wdown_ref, conv_ref,
                out_ref, act_ref, *, seg, stride):
    tm = TOKENS_PER_TILE
    x = x_ref[...].reshape(tm, D_MODEL)
    hb = _modulated_rmsnorm(x, gpre_ref[...], mod_ref[0, 3:4, :], mod_ref[0, 4:5, :]).astype(BF16)
    for f0 in range(0, D_FF, FF_BLOCK):
        ua = _pad_rows(jnp.dot(hb, wup_ref[:, f0:f0 + FF_BLOCK], preferred_element_type=F32))
        ub = _pad_rows(jnp.dot(hb, wup_ref[:, D_FF + f0:D_FF + f0 + FF_BLOCK], preferred_element_type=F32))
        wa = conv_ref[:, f0:f0 + FF_BLOCK]
        wb = conv_ref[:, D_FF + f0:D_FF + f0 + FF_BLOCK]
        for r0 in range(0, tm, ROW_BLOCK):
            ca = _conv3_block(ua, wa, seg, stride, r0, ROW_BLOCK)
            cb = _conv3_block(ub, wb, seg, stride, r0, ROW_BLOCK)
            act_ref[pl.ds(r0, ROW_BLOCK), pl.ds(f0, FF_BLOCK)] = (_silu(ca) * cb).astype(BF16)
    y = jnp.dot(act_ref[...], wdown_ref[...], preferred_element_type=F32)
    out = x + mod_ref[0, 5:6, :] * _rmsnorm(y, gpost_ref[...])
    out_ref[...] = out.reshape(out_ref.shape)


def _tile_view(a2d, on_grid, batch):
    width = a2d.shape[1]
    if on_grid:
        return a2d.reshape(batch, a2d.shape[0] // (batch * GRID_W), GRID_W, width)
    return a2d.reshape(a2d.shape[0] // TOKENS_PER_TILE, TOKENS_PER_TILE // COL_TILE_W, COL_TILE_W, width)


def _tile_spec(width, on_grid):
    block = (1, TOKENS_PER_TILE // COL_TILE_W, COL_TILE_W, width)
    if on_grid:
        per_batch = GRID_W // COL_TILE_W
        return pl.BlockSpec(block, lambda i: (i // per_batch, 0, i % per_batch, 0))
    return pl.BlockSpec(block, lambda i: (i, 0, 0, 0))


def _mod_spec(mod, on_grid):
    if on_grid:
        per_batch = GRID_W // COL_TILE_W
        return pl.BlockSpec((1, 6, D_MODEL), lambda i: (i // per_batch, 0, 0))
    return pl.BlockSpec((1, 6, D_MODEL), lambda i: (0, 0, 0))


def _v_conv_geometry(on_grid):
    return (TOKENS_PER_TILE, COL_TILE_W) if on_grid else (256, 1)


def _mixer_out(x2d, o_f, o_b, hc, cgxh, mod, p, *, on_grid, batch):
    tokens = x2d.shape[0]
    seg, stride = _v_conv_geometry(on_grid)
    consts = [p["g_pre_mix"], p["g_post_mix"], p["dn_norm_g"], p["w_z"], p["w_bg"], p["w_mgate"],
              p["w_dn_out"], p["w_cf_out"], p["w_sc_out"], p["w_o"], p["sc_conv"]]
    tiled = [x2d, o_f, o_b, hc, cgxh]
    out = pl.pallas_call(
        functools.partial(_mixer_out_kernel, seg=seg, stride=stride),
        out_shape=jax.ShapeDtypeStruct(_tile_view(x2d, on_grid, batch).shape, F32),
        grid=(tokens // TOKENS_PER_TILE,),
        in_specs=[_tile_spec(a.shape[1], on_grid) for a in tiled] + [_mod_spec(mod, on_grid)]
                 + [_const_spec(c.shape) for c in consts],
        out_specs=_tile_spec(D_MODEL, on_grid),
        scratch_shapes=[pltpu.VMEM((TOKENS_PER_TILE, D_MODEL), BF16)],
        compiler_params=pltpu.CompilerParams(dimension_semantics=("parallel",),
                                             vmem_limit_bytes=VMEM_LIMIT_BYTES),
        name="mixer_out",
    )(*[_tile_view(a, on_grid, batch) for a in tiled], mod, *consts)
    return out.reshape(tokens, D_MODEL)


def _ffn(x2d, mod, p, *, on_grid, batch):
    tokens = x2d.shape[0]
    seg, stride = _v_conv_geometry(on_grid)
    consts = [p["g_pre_ffn"], p["g_post_ffn"], p["w_ffn_up"], p["w_ffn_down"], p["ffn_conv"]]
    out = pl.pallas_call(
        functools.partial(_ffn_kernel, seg=seg, stride=stride),
        out_shape=jax.ShapeDtypeStruct(_tile_view(x2d, on_grid, batch).shape, F32),
        grid=(tokens // TOKENS_PER_TILE,),
        in_specs=[_tile_spec(D_MODEL, on_grid), _mod_spec(mod, on_grid)] + [_const_spec(c.shape) for c in consts],
        out_specs=_tile_spec(D_MODEL, on_grid),
        scratch_shapes=[pltpu.VMEM((TOKENS_PER_TILE, D_FF), BF16)],
        compiler_params=pltpu.CompilerParams(dimension_semantics=("parallel",),
                                             vmem_limit_bytes=VMEM_LIMIT_BYTES),
        name="ffn",
    )(_tile_view(x2d, on_grid, batch), mod, *consts)
    return out.reshape(tokens, D_MODEL)


_W_IN_PARTS = (("w_qkv", _OFF_Q, _OFF_Z), ("w_z", _OFF_Z, _OFF_A), ("w_dngate", _OFF_A, _OFF_CF),
               ("w_cf", _OFF_CF, _OFF_SC), ("w_bg", _OFF_SC, _OFF_SC + SC_WIDTH),
               ("w_sc", _OFF_SC + SC_WIDTH, _OFF_GATE), ("w_mgate", _OFF_GATE, _OFF_GATE + 3 * D_MODEL))


def _split_w_in_kernel(w_ref, *out_refs):
    w = w_ref[0]
    for out_ref, (_, lo, hi) in zip(out_refs, _W_IN_PARTS):
        out_ref[0] = w[:, lo:hi].astype(BF16)


def _split_w_in(w_in):
    depth, rows, cols = w_in.shape
    rb = 256
    widths = [hi - lo for _, lo, hi in _W_IN_PARTS]
    outs = pl.pallas_call(
        _split_w_in_kernel,
        out_shape=tuple(jax.ShapeDtypeStruct((depth, rows, width), BF16) for width in widths),
        grid=(depth, rows // rb),
        in_specs=[pl.BlockSpec((1, rb, cols), lambda l, r: (l, r, 0))],
        out_specs=tuple(pl.BlockSpec((1, rb, width), lambda l, r: (l, r, 0)) for width in widths),
        compiler_params=pltpu.CompilerParams(dimension_semantics=("parallel", "parallel"),
                                             vmem_limit_bytes=VMEM_LIMIT_BYTES),
        name="split_w_in",
    )(w_in)
    return {name: out for (name, _, _), out in zip(_W_IN_PARTS, outs)}


def _layer_params(l, w_in_parts, g_pre_mix, g_post_mix, g_pre_ffn, g_post_ffn, dn_conv, dn_a_log, dn_dt_bias,
                  dn_norm_g, w_dn_out, cf_conv, cf_ln_g, cf_ln_b, w_cf_out, sc_conv, w_sc_out, w_o,
                  w_ffn_up, ffn_conv, w_ffn_down):
    row = lambda v: v.reshape(1, -1).astype(F32)
    return {
        **{name: part[l] for name, part in w_in_parts.items()},
        "g_pre_mix": row(g_pre_mix[l]), "g_post_mix": row(g_post_mix[l]),
        "g_pre_ffn": row(g_pre_ffn[l]), "g_post_ffn": row(g_post_ffn[l]),
        "dn_conv": dn_conv[l], "dn_a_log": row(dn_a_log[l]), "dn_dt_bias": row(dn_dt_bias[l]),
        "dn_norm_g": row(dn_norm_g[l]), "w_dn_out": w_dn_out[l].astype(BF16),
        "cf_conv": cf_conv[l], "cf_ln_g": row(cf_ln_g[l]), "cf_ln_b": row(cf_ln_b[l]),
        "w_cf_out": w_cf_out[l].astype(BF16), "sc_conv": sc_conv[l], "w_sc_out": w_sc_out[l].astype(BF16),
        "w_o": w_o[l].astype(BF16), "w_ffn_up": w_ffn_up[l].astype(BF16), "ffn_conv": ffn_conv[l],
        "w_ffn_down": w_ffn_down[l].astype(BF16),
    }


def _trunk_layer(x2d, mod, p, s0, *, batch, seq_len, on_grid):
    seg_h = GRID_W if on_grid else seq_len
    qkv, pack, hc, cgxh = _mixer_in(x2d, mod, p, seq_len=seq_len, seg=seg_h)
    o_f, o_b, states = _deltanet(qkv, pack, s0, batch=batch, seq_len=seq_len)
    x2d = _mixer_out(x2d, o_f, o_b, hc, cgxh, mod, p, on_grid=on_grid, batch=batch)
    x2d = _ffn(x2d, mod, p, on_grid=on_grid, batch=batch)
    return x2d, states


def kernel(x_prompt, x_sample, state_dn, c, c_ctx, w_mod, b_mod, g_pre_mix, g_post_mix, g_pre_ffn, g_post_ffn,
           w_in, dn_conv, dn_a_log, dn_dt_bias, dn_norm_g, w_dn_out, cf_conv, cf_ln_g, cf_ln_b, w_cf_out,
           sc_conv, w_sc_out, w_o, w_ffn_up, ffn_conv, w_ffn_down):
    batch, seq, _ = x_prompt.shape
    dec_batch, dec_seq, _ = x_sample.shape
    assert dec_batch + 1 <= 8
    c_all = jnp.concatenate([c, c_ctx[None, :], jnp.zeros((8 - dec_batch - 1, D_MODEL), F32)], axis=0)
    mods = _modulation(c_all, w_mod, b_mod)
    xp = x_prompt.reshape(batch * seq, D_MODEL)
    xs = x_sample.reshape(dec_batch * dec_seq, D_MODEL)
    ctx_states = []
    w_in_parts = _split_w_in(w_in)
    for l in range(DEPTH):
        p = _layer_params(l, w_in_parts, g_pre_mix, g_post_mix, g_pre_ffn, g_post_ffn, dn_conv, dn_a_log,
                          dn_dt_bias, dn_norm_g, w_dn_out, cf_conv, cf_ln_g, cf_ln_b, w_cf_out, sc_conv, w_sc_out,
                          w_o, w_ffn_up, ffn_conv, w_ffn_down)
        mod_lat = mods[l, :dec_batch].reshape(dec_batch, 6, D_MODEL)
        mod_ctx = mods[l, dec_batch:dec_batch + 1].reshape(1, 6, D_MODEL)
        xp, st = _trunk_layer(xp, mod_ctx, p, None, batch=batch, seq_len=seq, on_grid=False)
        ctx_states.append(st)
        xs, _ = _trunk_layer(xs, mod_lat, p, state_dn[:, l].astype(F32), batch=dec_batch, seq_len=dec_seq,
                             on_grid=True)
    new_state = jnp.stack(ctx_states, axis=1).astype(x_prompt.dtype)
    return (xp.reshape(batch, seq, D_MODEL), xs.reshape(dec_batch, dec_seq, D_MODEL), new_state)
```

```python
import functools

import numpy as np
import jax
import jax.numpy as jnp
from jax import lax
from jax.experimental import pallas as pl
from jax.experimental.pallas import tpu as pltpu

F32 = jnp.float32
BF16 = jnp.bfloat16

D_MODEL = 1024
DEPTH = 2
GRID_W = 64
DN_HEADS = 4
DN_DK = 128
DN_DV = 128
DN_BLOCK = 128
DN_STEP_TOKENS = 256
DN_BATCH_PER_STEP = 2
DN_QK = DN_HEADS * DN_DK
DN_V = DN_HEADS * DN_DV
CF_WIDTH = 512
CF_CONV_W = 31
SC_WIDTH = 512
D_FF = 2816
EPS = 1e-6

_OFF_Q = 0
_OFF_Z = 2 * DN_QK + DN_V
_OFF_A = _OFF_Z + DN_V
_OFF_CF = _OFF_A + 4 * DN_HEADS
_OFF_SC = _OFF_CF + 2 * CF_WIDTH
_OFF_GATE = _OFF_SC + 3 * SC_WIDTH

SUBLANES = 8
BF16_SUBLANES = 16
LANES = 128
TOKENS_PER_TILE = 512
COL_TILE_W = 8
CONV_GAP = 16
ROW_BLOCK = 64
FF_BLOCK = 256
PACK_W = 128
VMEM_LIMIT_BYTES = 56 * 1024 * 1024


def _sigmoid(x):
    return 0.5 * jnp.tanh(0.5 * x) + 0.5


def _silu(x):
    h = 0.5 * x
    return h + h * jnp.tanh(h)


def _softplus(x):
    return jnp.maximum(x, 0.0) + jnp.log1p(jnp.exp(-jnp.abs(x)))


def _mm(a, b):
    return jnp.dot(a.astype(BF16), b.astype(BF16), preferred_element_type=F32)


def _mm_nt(a, b):
    return lax.dot_general(a.astype(BF16), b.astype(BF16), (((1,), (1,)), ((), ())),
                           preferred_element_type=F32)


def _modulated_rmsnorm(x, gain, shift, scale):
    y = x * lax.rsqrt(jnp.mean(x * x, axis=-1, keepdims=True) + EPS)
    return y * (gain * (1.0 + scale)) + shift


def _rmsnorm(x, gain):
    return x * lax.rsqrt(jnp.mean(x * x, axis=-1, keepdims=True) + EPS) * gain


def _store_segments(pad_ref, val, seg, gap):
    rows, width = val.shape
    zeros = jnp.zeros((gap, width), F32)
    pad_ref[pl.ds(0, gap), pl.ds(0, width)] = zeros
    for s in range(rows // seg):
        base = gap + s * (seg + gap)
        pad_ref[pl.ds(base, seg), pl.ds(0, width)] = val[s * seg:(s + 1) * seg]
        pad_ref[pl.ds(base + seg, gap), pl.ds(0, width)] = zeros


def _conv_block(pad_ref, w_ref, base, rows, c0, cols, w_c0):
    taps = w_ref.shape[0]
    offs = [k - taps // 2 for k in range(taps)]
    halo = SUBLANES * (-(-max(offs) // SUBLANES))
    n = rows + 2 * halo
    win = pad_ref[pl.ds(base - halo, n), pl.ds(c0, cols)]
    by_residue = {}
    for k, off in enumerate(offs):
        by_residue.setdefault(off % SUBLANES, []).append((k, off // SUBLANES))
    acc = None
    for b in sorted(by_residue):
        rolled = win if b == 0 else pltpu.roll(win, n - b, axis=0)
        for k, a in by_residue[b]:
            start = halo + SUBLANES * a
            term = rolled[start:start + rows] * w_ref[pl.ds(k, 1), pl.ds(w_c0, cols)]
            acc = term if acc is None else acc + term
    return acc


def _pad_rows(u):
    zeros = jnp.zeros((SUBLANES, u.shape[1]), u.dtype)
    return jnp.concatenate([zeros, u, zeros], axis=0)


def _conv3_block(u_pad, w, seg, stride, r0, rows):
    cur = u_pad[r0 + SUBLANES:r0 + SUBLANES + rows]
    if stride == SUBLANES:
        prev = u_pad[r0:r0 + rows]
        nxt = u_pad[r0 + 2 * SUBLANES:r0 + 2 * SUBLANES + rows]
    else:
        assert stride == 1 and seg % rows == 0
        n = rows + 2 * SUBLANES
        win = u_pad[r0:r0 + n]
        prev = pltpu.roll(win, 1, axis=0)[SUBLANES:SUBLANES + rows]
        nxt = pltpu.roll(win, n - 1, axis=0)[SUBLANES:SUBLANES + rows]
        sub = lax.broadcasted_iota(jnp.int32, (SUBLANES, u_pad.shape[1]), 0)
        if r0 % seg == 0:
            prev = jnp.concatenate([jnp.where(sub == 0, 0.0, prev[:SUBLANES]), prev[SUBLANES:]], axis=0)
        if (r0 + rows) % seg == 0:
            nxt = jnp.concatenate([nxt[:-SUBLANES], jnp.where(sub == SUBLANES - 1, 0.0, nxt[-SUBLANES:])], axis=0)
    return prev * w[0:1] + cur * w[1:2] + nxt * w[2:3]


def _mod_kernel(c_ref, w_ref, b_ref, o_ref):
    s = _silu(c_ref[...])
    o_ref[0] = _mm(s, w_ref[0]) + b_ref[0]


def _modulation(c_all, w_mod, b_mod):
    tn = 1024
    n_cols = 6 * D_MODEL
    return pl.pallas_call(
        _mod_kernel,
        out_shape=jax.ShapeDtypeStruct((DEPTH, 8, n_cols), F32),
        grid=(DEPTH, n_cols // tn),
        in_specs=[
            pl.BlockSpec((8, D_MODEL), lambda l, j: (0, 0)),
            pl.BlockSpec((1, D_MODEL, tn), lambda l, j: (l, 0, j)),
            pl.BlockSpec((1, 1, tn), lambda l, j: (l, 0, j)),
        ],
        out_specs=pl.BlockSpec((1, 8, tn), lambda l, j: (l, 0, j)),
        compiler_params=pltpu.CompilerParams(dimension_semantics=("parallel", "parallel"),
                                             vmem_limit_bytes=VMEM_LIMIT_BYTES),
        name="modulation",
    )(c_all, w_mod, b_mod.reshape(DEPTH, 1, n_cols))


def _mixer_in_kernel(x_ref, mod_ref, gpre_ref, wqkv_ref, wgate_ref, wcf_ref, wsc_ref,
                     dnconv_ref, cfconv_ref, lng_ref, lnb_ref, alog_ref, dtb_ref,
                     trif_ref, trib_ref, ones_ref,
                     qkv_out, pack_out, hc_out, cgxh_out,
                     pad_ref, *, seg):
    tm = x_ref.shape[0]
    nseg = tm // seg
    gap = CONV_GAP
    hb = _modulated_rmsnorm(x_ref[...], gpre_ref[...], mod_ref[0, 0:1, :], mod_ref[0, 1:2, :]).astype(BF16)

    for grp in range(3):
        proj = jnp.dot(hb, wqkv_ref[:, grp * DN_QK:(grp + 1) * DN_QK], preferred_element_type=F32)
        _store_segments(pad_ref, proj, seg, gap)
        for s in range(nseg):
            for r0 in range(0, seg, ROW_BLOCK):
                base = gap + s * (seg + gap) + r0
                rows = pl.ds(s * seg + r0, ROW_BLOCK)
                for hd in range(DN_HEADS):
                    col = grp * DN_QK + hd * DN_DK
                    a = _silu(_conv_block(pad_ref, dnconv_ref, base, ROW_BLOCK, hd * DN_DK, DN_DK, col))
                    if grp < 2:
                        inv_norm = lax.rsqrt(jnp.sum(a * a, axis=-1, keepdims=True) + EPS)
                        a = a * (inv_norm * DN_DK ** -0.5 if grp == 0 else inv_norm)
                    qkv_out[rows, pl.ds(col, DN_DK)] = a

    nh2 = 2 * DN_HEADS
    ab = jnp.dot(hb, wgate_ref[...], preferred_element_type=F32)
    g = -jnp.exp(alog_ref[...]) * _softplus(ab[:, 0:nh2] + dtb_ref[...])
    beta = _sigmoid(ab[:, nh2:2 * nh2])
    g1 = g.astype(BF16)
    r1 = g - g1.astype(F32)
    g2 = r1.astype(BF16)
    g3 = (r1 - g2.astype(F32)).astype(BF16)
    pieces = jnp.concatenate([g1, g2, g3], axis=1)

    def summed(mat_ref):
        c = jnp.dot(mat_ref[...], pieces, preferred_element_type=F32)
        return c[:, 0:nh2] + c[:, nh2:2 * nh2] + c[:, 2 * nh2:3 * nh2]

    lane = lax.broadcasted_iota(jnp.int32, (tm, nh2), 1)
    gam = jnp.where(lane < DN_HEADS, summed(trif_ref), summed(trib_ref))
    tot = summed(ones_ref)
    expg = jnp.exp(gam)
    pack_out[...] = jnp.concatenate(
        [gam, beta, expg, beta * expg, jnp.exp(tot - gam), jnp.exp(tot),
         jnp.zeros((tm, PACK_W - 6 * nh2), F32)], axis=1)

    pc = jnp.dot(hb, wcf_ref[...], preferred_element_type=F32)
    glu = pc[:, :CF_WIDTH] * _sigmoid(pc[:, CF_WIDTH:])
    _store_segments(pad_ref, glu, seg, gap)

    ps = jnp.dot(hb, wsc_ref[...], preferred_element_type=F32)
    cgxh_out[...] = ps[:, :SC_WIDTH] * ps[:, SC_WIDTH:]

    for s in range(nseg):
        for r0 in range(0, seg, ROW_BLOCK):
            base = gap + s * (seg + gap) + r0
            c = jnp.concatenate([_conv_block(pad_ref, cfconv_ref, base, ROW_BLOCK, c0, LANES, c0)
                                 for c0 in range(0, CF_WIDTH, LANES)], axis=1)
            mu = jnp.mean(c, axis=-1, keepdims=True)
            cc = c - mu
            y = cc * lax.rsqrt(jnp.mean(cc * cc, axis=-1, keepdims=True) + EPS)
            y = y * lng_ref[...] + lnb_ref[...]
            hc_out[pl.ds(s * seg + r0, ROW_BLOCK), :] = _silu(y)


def _chunk_matrices(tm):
    idx = np.arange(tm)
    same = (idx[:, None] // DN_BLOCK) == (idx[None, :] // DN_BLOCK)
    lower = same & (idx[:, None] >= idx[None, :])
    upper = same & (idx[:, None] <= idx[None, :])
    as_bf16 = lambda m: jnp.asarray(m.astype(np.float32), dtype=BF16)
    return as_bf16(lower), as_bf16(upper), as_bf16(same)


def _const_spec(operand):
    if isinstance(operand, tuple):
        stacked, layer = operand
        rest = stacked.ndim - 1
        return pl.BlockSpec((None,) + stacked.shape[1:], lambda *_: (layer,) + (0,) * rest,
                            pipeline_mode=pl.Buffered(1))
    nd = operand.ndim
    return pl.BlockSpec(operand.shape, lambda *_: (0,) * nd, pipeline_mode=pl.Buffered(1))


def _const_arg(operand):
    return operand[0] if isinstance(operand, tuple) else operand


def _mixer_in(x2d, mod, p, *, seq_len, seg):
    tokens = x2d.shape[0]
    tm = TOKENS_PER_TILE
    tiles_per_mod = max(seq_len // tm, 1) if mod.shape[0] > 1 else None
    mod_map = (lambda i: (i // tiles_per_mod, 0, 0)) if tiles_per_mod else (lambda i: (0, 0, 0))
    trif, trib, ones = _chunk_matrices(tm)
    pad_rows = CONV_GAP + (tm // seg) * (seg + CONV_GAP)
    row = lambda i: (i, 0)
    consts = [p["g_pre_mix"], p["w_qkv"], p["w_dngate"], p["w_cf"], p["w_sc"], p["dn_conv"], p["cf_conv"],
              p["cf_ln_g"], p["cf_ln_b"], p["dn_a_log"], p["dn_dt_bias"], trif, trib, ones]
    return pl.pallas_call(
        functools.partial(_mixer_in_kernel, seg=seg),
        out_shape=(jax.ShapeDtypeStruct((tokens, 2 * DN_QK + DN_V), F32),
                   jax.ShapeDtypeStruct((tokens, PACK_W), F32),
                   jax.ShapeDtypeStruct((tokens, CF_WIDTH), F32),
                   jax.ShapeDtypeStruct((tokens, SC_WIDTH), F32)),
        grid=(tokens // tm,),
        in_specs=[pl.BlockSpec((tm, D_MODEL), row), pl.BlockSpec((1, 6, D_MODEL), mod_map)]
                 + [_const_spec(c) for c in consts],
        out_specs=(pl.BlockSpec((tm, 2 * DN_QK + DN_V), row), pl.BlockSpec((tm, PACK_W), row),
                   pl.BlockSpec((tm, CF_WIDTH), row), pl.BlockSpec((tm, SC_WIDTH), row)),
        scratch_shapes=[pltpu.VMEM((pad_rows, DN_QK), F32)],
        compiler_params=pltpu.CompilerParams(dimension_semantics=("parallel",),
                                             vmem_limit_bytes=VMEM_LIMIT_BYTES),
        name="mixer_in",
    )(x2d, mod, *map(_const_arg, consts))


def _substitution_masks():
    n = DN_BLOCK
    i = np.arange(n)[:, None]
    j = np.arange(n)[None, :]
    out = []
    for strict in (i > j, i < j):
        levels = [strict & (i // 2 == j // 2)]
        size = 2
        while size < n:
            levels.append(strict & (i // (2 * size) == j // (2 * size)) & (i // size != j // size))
            size *= 2
        out.append(np.stack(levels))
    return jnp.asarray(np.stack(out).astype(np.float32), dtype=BF16)


def _deltanet_kernel(qf_ref, qb_ref, pf_ref, pb_ref, masks_ref, *rest, has_s0, nbatch, same_tokens):
    if has_s0:
        s0_ref, of_ref, ob_ref, sout_ref, state_ref = rest
    else:
        of_ref, ob_ref, sout_ref, state_ref = rest
    n = pl.program_id(1)
    last = pl.num_programs(1) - 1
    c = DN_BLOCK
    nh2 = 2 * DN_HEADS
    n_levels = masks_ref.shape[1]

    @pl.when(n == 0)
    def _():
        if has_s0:
            state_ref[...] = s0_ref[...]
        else:
            state_ref[...] = jnp.zeros(state_ref.shape, F32)

    ii = lax.broadcasted_iota(jnp.int32, (c, c), 0)
    jj = lax.broadcasted_iota(jnp.int32, (c, c), 1)
    eye = (ii == jj).astype(F32).astype(BF16)
    incl = ((ii >= jj), (ii <= jj))
    strict = ((ii > jj), (ii < jj))
    q_refs, p_refs, o_refs = (qf_ref, qb_ref), (pf_ref, pb_ref), (of_ref, ob_ref)

    nblk = DN_STEP_TOKENS // c
    chains = [(bi, d, hd) for bi in range(nbatch) for d in (0, 1) for hd in range(DN_HEADS)]
    probs = [(bi, d, blk, hd) for bi, d, hd in chains for blk in range(nblk)]
    rows = lambda blk: pl.ds(blk * c, c)
    pk = {(bi, d, blk): p_refs[d][bi, rows(blk), :]
          for bi in range(nbatch) for d in (0, 1) for blk in range(nblk)}
    pk_t = {key: val.T for key, val in pk.items()}

    def col(p, grp):
        bi, d, blk, hd = p
        j = grp * nh2 + d * DN_HEADS + hd
        return pk[(bi, d, blk)][:, j:j + 1]

    def head(p, off):
        bi, d, blk, hd = p
        return q_refs[d][bi, rows(blk), pl.ds(off + hd * DN_DK, DN_DK)]

    q = {p: head(p, 0) for p in probs}
    k = {p: head(p, DN_QK) for p in probs}
    v = {p: head(p, 2 * DN_QK) for p in probs}
    kb = {p: k[p].astype(BF16) for p in probs}
    kq = {}
    for p in probs:
        bi, d, blk, hd = p
        twin = (bi, 0, blk, hd)
        if same_tokens and d == 1:
            kq[p] = kq[twin]
        else:
            kq[p] = _mm_nt(jnp.concatenate([kb[p], q[p].astype(BF16)], axis=0), kb[p])
    kk = {p: kq[p][:c] for p in probs}
    e = {}
    for p in probs:
        bi, d, blk, hd = p
        gam_r = pk_t[(bi, d, blk)][d * DN_HEADS + hd:d * DN_HEADS + hd + 1, :]
        e[p] = jnp.exp(jnp.where(incl[d], col(p, 0) - gam_r, 0.0))
    a = {p: jnp.where(strict[p[1]], col(p, 1) * kk[p] * e[p], 0.0).astype(BF16) for p in probs}
    inv = {p: eye - a[p] * masks_ref[p[1], 0] for p in probs}
    for lvl in range(1, n_levels):
        size = 2 ** lvl
        off = {p: a[p] * masks_ref[p[1], lvl] for p in probs}
        if size % BF16_SUBLANES:
            t = {p: jnp.dot(inv[p], off[p], preferred_element_type=F32).astype(BF16) for p in probs}
            inv = {p: inv[p] - jnp.dot(t[p], inv[p], preferred_element_type=F32).astype(BF16) for p in probs}
            continue
        halves = {p: inv[p].reshape(c // (2 * size), 2, size, c) for p in probs}
        moving = {p: halves[p][:, 1 - p[1]].reshape(c // 2, c) for p in probs}
        t = {p: jnp.dot(moving[p], off[p], preferred_element_type=F32).astype(BF16) for p in probs}
        moved = {p: (moving[p] - jnp.dot(t[p], inv[p], preferred_element_type=F32).astype(BF16)
                     ).reshape(c // (2 * size), 1, size, c) for p in probs}
        inv = {p: jnp.concatenate([halves[p][:, 0:1], moved[p]] if p[1] == 0 else [moved[p], halves[p][:, 1:2]],
                                  axis=1).reshape(c, c) for p in probs}
    rhs = {p: jnp.concatenate([col(p, 1) * v[p], col(p, 3) * k[p]], axis=1).astype(BF16) for p in probs}
    sol = {p: jnp.dot(inv[p], rhs[p], preferred_element_type=F32) for p in probs}
    qkd = {p: (kq[p][c:] * jnp.where(incl[p[1]], e[p], 0.0)).astype(BF16) for p in probs}
    with_state = {p: jnp.concatenate([sol[p][:, DN_DV:].astype(BF16), (q[p] * col(p, 2)).astype(BF16)], axis=0)
                  for p in probs}
    with_v_new = {p: jnp.concatenate([qkd[p], (k[p] * col(p, 4)).T.astype(BF16)], axis=0) for p in probs}

    state = {ch: state_ref[ch] for ch in chains}
    for step in range(nblk):
        cur = {(bi, d, hd): (bi, d, step if d == 0 else nblk - 1 - step, hd) for bi, d, hd in chains}
        from_state = {ch: jnp.dot(with_state[cur[ch]], state[ch].astype(BF16), preferred_element_type=F32)
                      for ch in chains}
        v_new = {ch: (sol[cur[ch]][:, :DN_DV] - from_state[ch][:c]).astype(BF16) for ch in chains}
        from_v_new = {ch: jnp.dot(with_v_new[cur[ch]], v_new[ch], preferred_element_type=F32)
                      for ch in chains}
        new_state = {}
        for ch in chains:
            bi, d, blk, hd = cur[ch]
            o_refs[d][bi, rows(blk), pl.ds(hd * DN_DV, DN_DV)] = from_state[ch][c:] + from_v_new[ch][:c]
            j = 5 * nh2 + d * DN_HEADS + hd
            etot = pk[(bi, d, blk)][0:1, j:j + 1]
            new_state[ch] = state[ch] * etot + from_v_new[ch][c:]
        state = new_state
    for ch in chains:
        state_ref[ch] = state[ch]

    @pl.when(n == last)
    def _():
        sout_ref[...] = state_ref[...]


def _deltanet(qkv, pack, s0, *, batch, seq_len):
    tokens = qkv.shape[0]
    step = DN_STEP_TOKENS
    nb = seq_len // step
    nbatch = DN_BATCH_PER_STEP if batch % DN_BATCH_PER_STEP == 0 else 1
    width = 2 * DN_QK + DN_V
    fwd = lambda b, n: (b, n, 0)
    bwd = lambda b, n: (b, nb - 1 - n, 0)
    state_block = (nbatch, 2, DN_HEADS, DN_DK, DN_DV)
    state_map = lambda b, n: (b, 0, 0, 0, 0)
    masks = _substitution_masks()
    qkv3 = qkv.reshape(batch, seq_len, width)
    pack3 = pack.reshape(batch, seq_len, PACK_W)
    in_specs = [pl.BlockSpec((nbatch, step, width), fwd), pl.BlockSpec((nbatch, step, width), bwd),
                pl.BlockSpec((nbatch, step, PACK_W), fwd), pl.BlockSpec((nbatch, step, PACK_W), bwd),
                _const_spec(masks)]
    args = [qkv3, qkv3, pack3, pack3, masks]
    if s0 is not None:
        in_specs.append(pl.BlockSpec(state_block, state_map))
        args.append(s0)
    o_f, o_b, states = pl.pallas_call(
        functools.partial(_deltanet_kernel, has_s0=s0 is not None, nbatch=nbatch, same_tokens=nb == 1),
        out_shape=(jax.ShapeDtypeStruct((batch, seq_len, DN_V), F32),
                   jax.ShapeDtypeStruct((batch, seq_len, DN_V), F32),
                   jax.ShapeDtypeStruct((batch, 2, DN_HEADS, DN_DK, DN_DV), F32)),
        grid=(batch // nbatch, nb),
        in_specs=in_specs,
        out_specs=(pl.BlockSpec((nbatch, step, DN_V), fwd), pl.BlockSpec((nbatch, step, DN_V), bwd),
                   pl.BlockSpec(state_block, state_map)),
        scratch_shapes=[pltpu.VMEM(state_block, F32)],
        compiler_params=pltpu.CompilerParams(dimension_semantics=("parallel", "arbitrary"),
                                             vmem_limit_bytes=VMEM_LIMIT_BYTES),
        name="deltanet",
    )(*args)
    return o_f.reshape(tokens, DN_V), o_b.reshape(tokens, DN_V), states


def _mixer_out_kernel(x_ref, of_ref, ob_ref, hc_ref, cg_ref, mod_ref, gpre_ref, gpost_ref, dnn_ref,
                      wz_ref, wbg_ref, wmg_ref, wdn_ref, wcf_ref, wsc_ref, wo_ref, scconv_ref,
                      out_ref, min_ref, *, seg, stride):
    tm = TOKENS_PER_TILE
    x = x_ref[...].reshape(tm, D_MODEL)
    hb = _modulated_rmsnorm(x, gpre_ref[...], mod_ref[0, 0:1, :], mod_ref[0, 1:2, :]).astype(BF16)

    o = of_ref[...].reshape(tm, DN_V) + ob_ref[...].reshape(tm, DN_V)
    z = jnp.dot(hb, wz_ref[...], preferred_element_type=F32)
    heads = []
    for hd in range(DN_HEADS):
        oh = _rmsnorm(o[:, hd * DN_DV:(hd + 1) * DN_DV], dnn_ref[...])
        heads.append((oh * _silu(z[:, hd * DN_DV:(hd + 1) * DN_DV])).astype(BF16))
    o_gated = jnp.concatenate(heads, axis=1)

    cg_pad = _pad_rows(cg_ref[...].reshape(tm, SC_WIDTH))
    bg = jnp.dot(hb, wbg_ref[...], preferred_element_type=F32)
    w_sc = scconv_ref[...]
    sc_parts = [_conv3_block(cg_pad, w_sc, seg, stride, r0, ROW_BLOCK) * bg[r0:r0 + ROW_BLOCK]
                for r0 in range(0, tm, ROW_BLOCK)]
    sc_in = jnp.concatenate(sc_parts, axis=0).astype(BF16)
    hc = hc_ref[...].reshape(tm, CF_WIDTH).astype(BF16)

    cb = 256
    for c0 in range(0, D_MODEL, cb):
        cols = slice(c0, c0 + cb)
        y_dn = jnp.dot(o_gated, wdn_ref[:, cols], preferred_element_type=F32)
        y_cf = jnp.dot(hc, wcf_ref[:, cols], preferred_element_type=F32)
        y_sc = jnp.dot(sc_in, wsc_ref[:, cols], preferred_element_type=F32)
        ga = _sigmoid(jnp.dot(hb, wmg_ref[:, c0:c0 + cb], preferred_element_type=F32))
        gb = _sigmoid(jnp.dot(hb, wmg_ref[:, D_MODEL + c0:D_MODEL + c0 + cb], preferred_element_type=F32))
        gc = _sigmoid(jnp.dot(hb, wmg_ref[:, 2 * D_MODEL + c0:2 * D_MODEL + c0 + cb], preferred_element_type=F32))
        min_ref[:, cols] = (ga * y_dn + gb * y_cf + gc * y_sc).astype(BF16)

    m = jnp.dot(min_ref[...], wo_ref[...], preferred_element_type=F32)
    out = x + mod_ref[0, 2:3, :] * _rmsnorm(m, gpost_ref[...])
    out_ref[...] = out.reshape(out_ref.shape)


def _ffn_kernel(x_ref, mod_ref, gpre_ref, gpost_ref, wup_ref, wdown_ref, conv_ref,
                out_ref, act_ref, *, seg, stride):
    tm = TOKENS_PER_TILE
    x = x_ref[...].reshape(tm, D_MODEL)
    hb = _modulated_rmsnorm(x, gpre_ref[...], mod_ref[0, 3:4, :], mod_ref[0, 4:5, :]).astype(BF16)
    for f0 in range(0, D_FF, FF_BLOCK):
        ua = _pad_rows(jnp.dot(hb, wup_ref[:, f0:f0 + FF_BLOCK], preferred_element_type=F32))
        ub = _pad_rows(jnp.dot(hb, wup_ref[:, D_FF + f0:D_FF + f0 + FF_BLOCK], preferred_element_type=F32))
        wa = conv_ref[:, f0:f0 + FF_BLOCK]
        wb = conv_ref[:, D_FF + f0:D_FF + f0 + FF_BLOCK]
        for r0 in range(0, tm, ROW_BLOCK):
            ca = _conv3_block(ua, wa, seg, stride, r0, ROW_BLOCK)
            cb = _conv3_block(ub, wb, seg, stride, r0, ROW_BLOCK)
            act_ref[pl.ds(r0, ROW_BLOCK), pl.ds(f0, FF_BLOCK)] = (_silu(ca) * cb).astype(BF16)
    y = jnp.dot(act_ref[...], wdown_ref[...], preferred_element_type=F32)
    out = x + mod_ref[0, 5:6, :] * _rmsnorm(y, gpost_ref[...])
    out_ref[...] = out.reshape(out_ref.shape)


def _tile_view(a2d, on_grid, batch):
    width = a2d.shape[1]
    if on_grid:
        return a2d.reshape(batch, a2d.shape[0] // (batch * GRID_W), GRID_W, width)
    return a2d.reshape(a2d.shape[0] // TOKENS_PER_TILE, TOKENS_PER_TILE // COL_TILE_W, COL_TILE_W, width)


def _tile_spec(width, on_grid):
    block = (1, TOKENS_PER_TILE // COL_TILE_W, COL_TILE_W, width)
    if on_grid:
        per_batch = GRID_W // COL_TILE_W
        return pl.BlockSpec(block, lambda i: (i // per_batch, 0, i % per_batch, 0))
    return pl.BlockSpec(block, lambda i: (i, 0, 0, 0))


def _mod_spec(mod, on_grid):
    if on_grid:
        per_batch = GRID_W // COL_TILE_W
        return pl.BlockSpec((1, 6, D_MODEL), lambda i: (i // per_batch, 0, 0))
    return pl.BlockSpec((1, 6, D_MODEL), lambda i: (0, 0, 0))


def _v_conv_geometry(on_grid):
    return (TOKENS_PER_TILE, COL_TILE_W) if on_grid else (256, 1)


def _mixer_out(x2d, o_f, o_b, hc, cgxh, mod, p, *, on_grid, batch):
    tokens = x2d.shape[0]
    seg, stride = _v_conv_geometry(on_grid)
    consts = [p["g_pre_mix"], p["g_post_mix"], p["dn_norm_g"], p["w_z"], p["w_bg"], p["w_mgate"],
              p["w_dn_out"], p["w_cf_out"], p["w_sc_out"], p["w_o"], p["sc_conv"]]
    tiled = [x2d, o_f, o_b, hc, cgxh]
    out = pl.pallas_call(
        functools.partial(_mixer_out_kernel, seg=seg, stride=stride),
        out_shape=jax.ShapeDtypeStruct(_tile_view(x2d, on_grid, batch).shape, F32),
        grid=(tokens // TOKENS_PER_TILE,),
        in_specs=[_tile_spec(a.shape[1], on_grid) for a in tiled] + [_mod_spec(mod, on_grid)]
                 + [_const_spec(c) for c in consts],
        out_specs=_tile_spec(D_MODEL, on_grid),
        scratch_shapes=[pltpu.VMEM((TOKENS_PER_TILE, D_MODEL), BF16)],
        compiler_params=pltpu.CompilerParams(dimension_semantics=("parallel",),
                                             vmem_limit_bytes=VMEM_LIMIT_BYTES),
        name="mixer_out",
    )(*[_tile_view(a, on_grid, batch) for a in tiled], mod, *map(_const_arg, consts))
    return out.reshape(tokens, D_MODEL)


def _ffn(x2d, mod, p, *, on_grid, batch):
    tokens = x2d.shape[0]
    seg, stride = _v_conv_geometry(on_grid)
    consts = [p["g_pre_ffn"], p["g_post_ffn"], p["w_ffn_up"], p["w_ffn_down"], p["ffn_conv"]]
    out = pl.pallas_call(
        functools.partial(_ffn_kernel, seg=seg, stride=stride),
        out_shape=jax.ShapeDtypeStruct(_tile_view(x2d, on_grid, batch).shape, F32),
        grid=(tokens // TOKENS_PER_TILE,),
        in_specs=[_tile_spec(D_MODEL, on_grid), _mod_spec(mod, on_grid)] + [_const_spec(c) for c in consts],
        out_specs=_tile_spec(D_MODEL, on_grid),
        scratch_shapes=[pltpu.VMEM((TOKENS_PER_TILE, D_FF), BF16)],
        compiler_params=pltpu.CompilerParams(dimension_semantics=("parallel",),
                                             vmem_limit_bytes=VMEM_LIMIT_BYTES),
        name="ffn",
    )(_tile_view(x2d, on_grid, batch), mod, *map(_const_arg, consts))
    return out.reshape(tokens, D_MODEL)


_W_IN_PARTS = (("w_qkv", _OFF_Q, _OFF_Z), ("w_z", _OFF_Z, _OFF_A), ("w_dngate", _OFF_A, _OFF_CF),
               ("w_cf", _OFF_CF, _OFF_SC), ("w_bg", _OFF_SC, _OFF_SC + SC_WIDTH),
               ("w_sc", _OFF_SC + SC_WIDTH, _OFF_GATE), ("w_mgate", _OFF_GATE, _OFF_GATE + 3 * D_MODEL))


def _split_w_in_kernel(wt_ref, *out_refs, parts):
    wt = wt_ref[0]
    for out_ref, (_, lo, hi) in zip(out_refs, parts):
        out_ref[0] = wt[lo:hi, :].T.astype(BF16)


def _split_w_in(w_in):
    depth, rows, cols = w_in.shape
    rb = 256
    parts = [part for part in _W_IN_PARTS if (part[2] - part[1]) % LANES == 0]
    assert all(lo % SUBLANES == 0 for _, lo, _ in parts)
    widths = [hi - lo for _, lo, hi in parts]
    outs = pl.pallas_call(
        functools.partial(_split_w_in_kernel, parts=parts),
        out_shape=tuple(jax.ShapeDtypeStruct((depth, rows, width), BF16) for width in widths),
        grid=(depth, rows // rb),
        in_specs=[pl.BlockSpec((1, cols, rb), lambda l, r: (l, 0, r))],
        out_specs=tuple(pl.BlockSpec((1, rb, width), lambda l, r: (l, r, 0)) for width in widths),
        compiler_params=pltpu.CompilerParams(dimension_semantics=("parallel", "parallel"),
                                             vmem_limit_bytes=VMEM_LIMIT_BYTES),
        name="split_w_in",
    )(jnp.swapaxes(w_in, 1, 2))
    split = {name: out for (name, _, _), out in zip(parts, outs)}
    for name, lo, hi in _W_IN_PARTS:
        if name not in split:
            split[name] = w_in[:, :, lo:hi].astype(BF16)
    return split


def _matrix_weights(w_in, w_dn_out, w_cf_out, w_sc_out, w_o, w_ffn_up, w_ffn_down):
    stacked = _split_w_in(w_in)
    stacked.update(w_dn_out=w_dn_out.astype(BF16), w_cf_out=w_cf_out.astype(BF16), w_sc_out=w_sc_out.astype(BF16),
                   w_o=w_o.astype(BF16), w_ffn_up=w_ffn_up.astype(BF16), w_ffn_down=w_ffn_down.astype(BF16))
    return stacked


def _layer_params(l, matrices, g_pre_mix, g_post_mix, g_pre_ffn, g_post_ffn, dn_conv, dn_a_log, dn_dt_bias,
                  dn_norm_g, cf_conv, cf_ln_g, cf_ln_b, sc_conv, ffn_conv):
    row = lambda v: v.reshape(1, -1).astype(F32)
    return {
        **{name: (stacked, l) for name, stacked in matrices.items()},
        "g_pre_mix": row(g_pre_mix[l]), "g_post_mix": row(g_post_mix[l]),
        "g_pre_ffn": row(g_pre_ffn[l]), "g_post_ffn": row(g_post_ffn[l]),
        "dn_conv": dn_conv[l], "dn_a_log": row(dn_a_log[l]), "dn_dt_bias": row(dn_dt_bias[l]),
        "dn_norm_g": row(dn_norm_g[l]), "cf_conv": cf_conv[l], "cf_ln_g": row(cf_ln_g[l]),
        "cf_ln_b": row(cf_ln_b[l]), "sc_conv": sc_conv[l], "ffn_conv": ffn_conv[l],
    }


def _trunk_layer(x2d, mod, p, s0, *, batch, seq_len, on_grid):
    seg_h = GRID_W if on_grid else seq_len
    qkv, pack, hc, cgxh = _mixer_in(x2d, mod, p, seq_len=seq_len, seg=seg_h)
    o_f, o_b, states = _deltanet(qkv, pack, s0, batch=batch, seq_len=seq_len)
    x2d = _mixer_out(x2d, o_f, o_b, hc, cgxh, mod, p, on_grid=on_grid, batch=batch)
    x2d = _ffn(x2d, mod, p, on_grid=on_grid, batch=batch)
    return x2d, states


def kernel(x_prompt, x_sample, state_dn, c, c_ctx, w_mod, b_mod, g_pre_mix, g_post_mix, g_pre_ffn, g_post_ffn,
           w_in, dn_conv, dn_a_log, dn_dt_bias, dn_norm_g, w_dn_out, cf_conv, cf_ln_g, cf_ln_b, w_cf_out,
           sc_conv, w_sc_out, w_o, w_ffn_up, ffn_conv, w_ffn_down):
    batch, seq, _ = x_prompt.shape
    dec_batch, dec_seq, _ = x_sample.shape
    assert dec_batch + 1 <= 8
    c_all = jnp.concatenate([c, c_ctx[None, :], jnp.zeros((8 - dec_batch - 1, D_MODEL), F32)], axis=0)
    mods = _modulation(c_all, w_mod, b_mod)
    xp = x_prompt.reshape(batch * seq, D_MODEL)
    xs = x_sample.reshape(dec_batch * dec_seq, D_MODEL)
    ctx_states = []
    matrices = _matrix_weights(w_in, w_dn_out, w_cf_out, w_sc_out, w_o, w_ffn_up, w_ffn_down)
    for l in range(DEPTH):
        p = _layer_params(l, matrices, g_pre_mix, g_post_mix, g_pre_ffn, g_post_ffn, dn_conv, dn_a_log,
                          dn_dt_bias, dn_norm_g, cf_conv, cf_ln_g, cf_ln_b, sc_conv, ffn_conv)
        mod_lat = mods[l, :dec_batch].reshape(dec_batch, 6, D_MODEL)
        mod_ctx = mods[l, dec_batch:dec_batch + 1].reshape(1, 6, D_MODEL)
        xp, st = _trunk_layer(xp, mod_ctx, p, None, batch=batch, seq_len=seq, on_grid=False)
        ctx_states.append(st)
        xs, _ = _trunk_layer(xs, mod_lat, p, state_dn[:, l].astype(F32), batch=dec_batch, seq_len=dec_seq,
                             on_grid=True)
    new_state = jnp.stack(ctx_states, axis=1).astype(x_prompt.dtype)
    return (xp.reshape(batch, seq, D_MODEL), xs.reshape(dec_batch, dec_seq, D_MODEL), new_state)
```

```python
import functools

import numpy as np
import jax
import jax.numpy as jnp
from jax import lax
from jax.experimental import pallas as pl
from jax.experimental.pallas import tpu as pltpu

F32 = jnp.float32
BF16 = jnp.bfloat16

D_MODEL = 1024
DEPTH = 2
GRID_W = 64
DN_HEADS = 4
DN_DK = 128
DN_DV = 128
DN_BLOCK = 128
DN_STEP_TOKENS = 256
DN_BATCH_PER_STEP = 2
DN_QK = DN_HEADS * DN_DK
DN_V = DN_HEADS * DN_DV
CF_WIDTH = 512
CF_CONV_W = 31
SC_WIDTH = 512
D_FF = 2816
EPS = 1e-6

_OFF_Q = 0
_OFF_Z = 2 * DN_QK + DN_V
_OFF_A = _OFF_Z + DN_V
_OFF_CF = _OFF_A + 4 * DN_HEADS
_OFF_SC = _OFF_CF + 2 * CF_WIDTH
_OFF_GATE = _OFF_SC + 3 * SC_WIDTH

SUBLANES = 8
BF16_SUBLANES = 16
LANES = 128
TOKENS_PER_TILE = 512
COL_TILE_W = 8
CONV_GAP = 16
ROW_BLOCK = 64
FF_BLOCK = 256
PACK_W = 128
VMEM_LIMIT_BYTES = 56 * 1024 * 1024


def _sigmoid(x):
    return 0.5 * jnp.tanh(0.5 * x) + 0.5


def _silu(x):
    h = 0.5 * x
    return h + h * jnp.tanh(h)


def _softplus(x):
    return jnp.maximum(x, 0.0) + jnp.log1p(jnp.exp(-jnp.abs(x)))


def _mm(a, b):
    return jnp.dot(a.astype(BF16), b.astype(BF16), preferred_element_type=F32)


def _mm_nt(a, b):
    return lax.dot_general(a.astype(BF16), b.astype(BF16), (((1,), (1,)), ((), ())),
                           preferred_element_type=F32)


def _modulated_rmsnorm(x, gain, shift, scale):
    y = x * lax.rsqrt(jnp.mean(x * x, axis=-1, keepdims=True) + EPS)
    return y * (gain * (1.0 + scale)) + shift


def _rmsnorm(x, gain):
    return x * lax.rsqrt(jnp.mean(x * x, axis=-1, keepdims=True) + EPS) * gain


def _store_segments(pad_ref, val, seg, gap):
    rows, width = val.shape
    zeros = jnp.zeros((gap, width), F32)
    pad_ref[pl.ds(0, gap), pl.ds(0, width)] = zeros
    for s in range(rows // seg):
        base = gap + s * (seg + gap)
        pad_ref[pl.ds(base, seg), pl.ds(0, width)] = val[s * seg:(s + 1) * seg]
        pad_ref[pl.ds(base + seg, gap), pl.ds(0, width)] = zeros


def _conv_block(pad_ref, w_ref, base, rows, c0, cols, w_c0):
    taps = w_ref.shape[0]
    offs = [k - taps // 2 for k in range(taps)]
    halo = SUBLANES * (-(-max(offs) // SUBLANES))
    n = rows + 2 * halo
    win = pad_ref[pl.ds(base - halo, n), pl.ds(c0, cols)]
    by_residue = {}
    for k, off in enumerate(offs):
        by_residue.setdefault(off % SUBLANES, []).append((k, off // SUBLANES))
    acc = None
    for b in sorted(by_residue):
        rolled = win if b == 0 else pltpu.roll(win, n - b, axis=0)
        for k, a in by_residue[b]:
            start = halo + SUBLANES * a
            term = rolled[start:start + rows] * w_ref[pl.ds(k, 1), pl.ds(w_c0, cols)]
            acc = term if acc is None else acc + term
    return acc


def _pad_rows(u):
    zeros = jnp.zeros((SUBLANES, u.shape[1]), u.dtype)
    return jnp.concatenate([zeros, u, zeros], axis=0)


def _conv3_block(u_pad, w, seg, stride, r0, rows):
    cur = u_pad[r0 + SUBLANES:r0 + SUBLANES + rows]
    if stride == SUBLANES:
        prev = u_pad[r0:r0 + rows]
        nxt = u_pad[r0 + 2 * SUBLANES:r0 + 2 * SUBLANES + rows]
    else:
        assert stride == 1 and seg % rows == 0
        n = rows + 2 * SUBLANES
        win = u_pad[r0:r0 + n]
        prev = pltpu.roll(win, 1, axis=0)[SUBLANES:SUBLANES + rows]
        nxt = pltpu.roll(win, n - 1, axis=0)[SUBLANES:SUBLANES + rows]
        sub = lax.broadcasted_iota(jnp.int32, (SUBLANES, u_pad.shape[1]), 0)
        if r0 % seg == 0:
            prev = jnp.concatenate([jnp.where(sub == 0, 0.0, prev[:SUBLANES]), prev[SUBLANES:]], axis=0)
        if (r0 + rows) % seg == 0:
            nxt = jnp.concatenate([nxt[:-SUBLANES], jnp.where(sub == SUBLANES - 1, 0.0, nxt[-SUBLANES:])], axis=0)
    return prev * w[0:1] + cur * w[1:2] + nxt * w[2:3]


def _mod_kernel(c_ref, w_ref, b_ref, o_ref):
    s = _silu(c_ref[...])
    o_ref[0] = _mm(s, w_ref[0]) + b_ref[0]


def _modulation(c_all, w_mod, b_mod):
    tn = 1024
    n_cols = 6 * D_MODEL
    return pl.pallas_call(
        _mod_kernel,
        out_shape=jax.ShapeDtypeStruct((DEPTH, 8, n_cols), F32),
        grid=(DEPTH, n_cols // tn),
        in_specs=[
            pl.BlockSpec((8, D_MODEL), lambda l, j: (0, 0)),
            pl.BlockSpec((1, D_MODEL, tn), lambda l, j: (l, 0, j)),
            pl.BlockSpec((1, 1, tn), lambda l, j: (l, 0, j)),
        ],
        out_specs=pl.BlockSpec((1, 8, tn), lambda l, j: (l, 0, j)),
        compiler_params=pltpu.CompilerParams(dimension_semantics=("parallel", "parallel"),
                                             vmem_limit_bytes=VMEM_LIMIT_BYTES),
        name="modulation",
    )(c_all, w_mod, b_mod.reshape(DEPTH, 1, n_cols))


def _mixer_in_kernel(x_ref, mod_ref, gpre_ref, wqkv_ref, wgate_ref, wcf_ref, wsc_ref,
                     dnconv_ref, cfconv_ref, lng_ref, lnb_ref, alog_ref, dtb_ref,
                     trif_ref, trib_ref, ones_ref,
                     qkv_out, pack_out, hc_out, cgxh_out,
                     pad_ref, *, seg):
    tm = x_ref.shape[0]
    nseg = tm // seg
    gap = CONV_GAP
    hb = _modulated_rmsnorm(x_ref[...], gpre_ref[...], mod_ref[0, 0:1, :], mod_ref[0, 1:2, :]).astype(BF16)

    for grp in range(3):
        proj = jnp.dot(hb, wqkv_ref[:, grp * DN_QK:(grp + 1) * DN_QK], preferred_element_type=F32)
        _store_segments(pad_ref, proj, seg, gap)
        for s in range(nseg):
            for r0 in range(0, seg, ROW_BLOCK):
                base = gap + s * (seg + gap) + r0
                rows = pl.ds(s * seg + r0, ROW_BLOCK)
                for hd in range(DN_HEADS):
                    col = grp * DN_QK + hd * DN_DK
                    a = _silu(_conv_block(pad_ref, dnconv_ref, base, ROW_BLOCK, hd * DN_DK, DN_DK, col))
                    if grp < 2:
                        inv_norm = lax.rsqrt(jnp.sum(a * a, axis=-1, keepdims=True) + EPS)
                        a = a * (inv_norm * DN_DK ** -0.5 if grp == 0 else inv_norm)
                    qkv_out[rows, pl.ds(col, DN_DK)] = a

    nh2 = 2 * DN_HEADS
    ab = jnp.dot(hb, wgate_ref[...], preferred_element_type=F32)
    g = -jnp.exp(alog_ref[...]) * _softplus(ab[:, 0:nh2] + dtb_ref[...])
    beta = _sigmoid(ab[:, nh2:2 * nh2])
    g1 = g.astype(BF16)
    r1 = g - g1.astype(F32)
    g2 = r1.astype(BF16)
    g3 = (r1 - g2.astype(F32)).astype(BF16)
    pieces = jnp.concatenate([g1, g2, g3], axis=1)

    def summed(mat_ref):
        c = jnp.dot(mat_ref[...], pieces, preferred_element_type=F32)
        return c[:, 0:nh2] + c[:, nh2:2 * nh2] + c[:, 2 * nh2:3 * nh2]

    lane = lax.broadcasted_iota(jnp.int32, (tm, nh2), 1)
    gam = jnp.where(lane < DN_HEADS, summed(trif_ref), summed(trib_ref))
    tot = summed(ones_ref)
    expg = jnp.exp(gam)
    pack_out[...] = jnp.concatenate(
        [gam, beta, expg, beta * expg, jnp.exp(tot - gam), jnp.exp(tot),
         jnp.zeros((tm, PACK_W - 6 * nh2), F32)], axis=1)

    pc = jnp.dot(hb, wcf_ref[...], preferred_element_type=F32)
    glu = pc[:, :CF_WIDTH] * _sigmoid(pc[:, CF_WIDTH:])
    _store_segments(pad_ref, glu, seg, gap)

    ps = jnp.dot(hb, wsc_ref[...], preferred_element_type=F32)
    cgxh_out[...] = ps[:, :SC_WIDTH] * ps[:, SC_WIDTH:]

    for s in range(nseg):
        for r0 in range(0, seg, ROW_BLOCK):
            base = gap + s * (seg + gap) + r0
            c = jnp.concatenate([_conv_block(pad_ref, cfconv_ref, base, ROW_BLOCK, c0, LANES, c0)
                                 for c0 in range(0, CF_WIDTH, LANES)], axis=1)
            mu = jnp.mean(c, axis=-1, keepdims=True)
            cc = c - mu
            y = cc * lax.rsqrt(jnp.mean(cc * cc, axis=-1, keepdims=True) + EPS)
            y = y * lng_ref[...] + lnb_ref[...]
            hc_out[pl.ds(s * seg + r0, ROW_BLOCK), :] = _silu(y)


def _chunk_matrices(tm):
    idx = np.arange(tm)
    same = (idx[:, None] // DN_BLOCK) == (idx[None, :] // DN_BLOCK)
    lower = same & (idx[:, None] >= idx[None, :])
    upper = same & (idx[:, None] <= idx[None, :])
    as_bf16 = lambda m: jnp.asarray(m.astype(np.float32), dtype=BF16)
    return as_bf16(lower), as_bf16(upper), as_bf16(same)


def _const_spec(operand):
    if isinstance(operand, tuple):
        stacked, layer = operand
        rest = stacked.ndim - 1
        return pl.BlockSpec((None,) + stacked.shape[1:], lambda *_: (layer,) + (0,) * rest,
                            pipeline_mode=pl.Buffered(1))
    nd = operand.ndim
    return pl.BlockSpec(operand.shape, lambda *_: (0,) * nd, pipeline_mode=pl.Buffered(1))


def _const_arg(operand):
    return operand[0] if isinstance(operand, tuple) else operand


def _mixer_in(x2d, mod, p, *, seq_len, seg):
    tokens = x2d.shape[0]
    tm = TOKENS_PER_TILE
    tiles_per_mod = max(seq_len // tm, 1) if mod.shape[0] > 1 else None
    mod_map = (lambda i: (i // tiles_per_mod, 0, 0)) if tiles_per_mod else (lambda i: (0, 0, 0))
    trif, trib, ones = _chunk_matrices(tm)
    pad_rows = CONV_GAP + (tm // seg) * (seg + CONV_GAP)
    row = lambda i: (i, 0)
    consts = [p["g_pre_mix"], p["w_qkv"], p["w_dngate"], p["w_cf"], p["w_sc"], p["dn_conv"], p["cf_conv"],
              p["cf_ln_g"], p["cf_ln_b"], p["dn_a_log"], p["dn_dt_bias"], trif, trib, ones]
    return pl.pallas_call(
        functools.partial(_mixer_in_kernel, seg=seg),
        out_shape=(jax.ShapeDtypeStruct((tokens, 2 * DN_QK + DN_V), F32),
                   jax.ShapeDtypeStruct((tokens, PACK_W), F32),
                   jax.ShapeDtypeStruct((tokens, CF_WIDTH), F32),
                   jax.ShapeDtypeStruct((tokens, SC_WIDTH), F32)),
        grid=(tokens // tm,),
        in_specs=[pl.BlockSpec((tm, D_MODEL), row), pl.BlockSpec((1, 6, D_MODEL), mod_map)]
                 + [_const_spec(c) for c in consts],
        out_specs=(pl.BlockSpec((tm, 2 * DN_QK + DN_V), row), pl.BlockSpec((tm, PACK_W), row),
                   pl.BlockSpec((tm, CF_WIDTH), row), pl.BlockSpec((tm, SC_WIDTH), row)),
        scratch_shapes=[pltpu.VMEM((pad_rows, DN_QK), F32)],
        compiler_params=pltpu.CompilerParams(dimension_semantics=("parallel",),
                                             vmem_limit_bytes=VMEM_LIMIT_BYTES),
        name="mixer_in",
    )(x2d, mod, *map(_const_arg, consts))


def _substitution_masks():
    n = DN_BLOCK
    i = np.arange(n)[:, None]
    j = np.arange(n)[None, :]
    out = []
    for strict in (i > j, i < j):
        levels = [strict & (i // 2 == j // 2)]
        size = 2
        while size < n:
            levels.append(strict & (i // (2 * size) == j // (2 * size)) & (i // size != j // size))
            size *= 2
        out.append(np.stack(levels))
    return jnp.asarray(np.stack(out).astype(np.float32), dtype=BF16)


def _deltanet_kernel(qf_ref, qb_ref, pf_ref, pb_ref, masks_ref, *rest, has_s0, nbatch, same_tokens):
    if has_s0:
        s0_ref, of_ref, ob_ref, sout_ref, state_ref = rest
    else:
        of_ref, ob_ref, sout_ref, state_ref = rest
    n = pl.program_id(1)
    last = pl.num_programs(1) - 1
    c = DN_BLOCK
    nh2 = 2 * DN_HEADS
    n_levels = masks_ref.shape[1]

    @pl.when(n == 0)
    def _():
        if has_s0:
            state_ref[...] = s0_ref[...]
        else:
            state_ref[...] = jnp.zeros(state_ref.shape, F32)

    ii = lax.broadcasted_iota(jnp.int32, (c, c), 0)
    jj = lax.broadcasted_iota(jnp.int32, (c, c), 1)
    eye = (ii == jj).astype(F32).astype(BF16)
    incl = ((ii >= jj), (ii <= jj))
    strict = ((ii > jj), (ii < jj))
    q_refs, p_refs, o_refs = (qf_ref, qb_ref), (pf_ref, pb_ref), (of_ref, ob_ref)

    nblk = qf_ref.shape[1] // c
    chains = [(bi, d, hd) for bi in range(nbatch) for d in (0, 1) for hd in range(DN_HEADS)]
    probs = [(bi, d, blk, hd) for bi, d, hd in chains for blk in range(nblk)]
    rows = lambda blk: pl.ds(blk * c, c)
    pk = {(bi, d, blk): p_refs[d][bi, rows(blk), :]
          for bi in range(nbatch) for d in (0, 1) for blk in range(nblk)}
    pk_t = {key: val.T for key, val in pk.items()}

    def col(p, grp):
        bi, d, blk, hd = p
        j = grp * nh2 + d * DN_HEADS + hd
        return pk[(bi, d, blk)][:, j:j + 1]

    def head(p, off):
        bi, d, blk, hd = p
        return q_refs[d][bi, rows(blk), pl.ds(off + hd * DN_DK, DN_DK)]

    q = {p: head(p, 0) for p in probs}
    k = {p: head(p, DN_QK) for p in probs}
    v = {p: head(p, 2 * DN_QK) for p in probs}
    kb = {p: k[p].astype(BF16) for p in probs}
    kq = {}
    for p in probs:
        bi, d, blk, hd = p
        twin = (bi, 0, blk, hd)
        if same_tokens and d == 1:
            kq[p] = kq[twin]
        else:
            kq[p] = _mm_nt(jnp.concatenate([kb[p], q[p].astype(BF16)], axis=0), kb[p])
    kk = {p: kq[p][:c] for p in probs}
    e = {}
    for p in probs:
        bi, d, blk, hd = p
        gam_r = pk_t[(bi, d, blk)][d * DN_HEADS + hd:d * DN_HEADS + hd + 1, :]
        e[p] = jnp.exp(jnp.where(incl[d], col(p, 0) - gam_r, 0.0))
    a = {p: jnp.where(strict[p[1]], col(p, 1) * kk[p] * e[p], 0.0).astype(BF16) for p in probs}
    inv = {p: eye - a[p] * masks_ref[p[1], 0] for p in probs}
    for lvl in range(1, n_levels):
        size = 2 ** lvl
        off = {p: a[p] * masks_ref[p[1], lvl] for p in probs}
        if size % BF16_SUBLANES:
            t = {p: jnp.dot(inv[p], off[p], preferred_element_type=F32).astype(BF16) for p in probs}
            inv = {p: inv[p] - jnp.dot(t[p], inv[p], preferred_element_type=F32).astype(BF16) for p in probs}
            continue
        halves = {p: inv[p].reshape(c // (2 * size), 2, size, c) for p in probs}
        moving = {p: halves[p][:, 1 - p[1]].reshape(c // 2, c) for p in probs}
        t = {p: jnp.dot(moving[p], off[p], preferred_element_type=F32).astype(BF16) for p in probs}
        moved = {p: (moving[p] - jnp.dot(t[p], inv[p], preferred_element_type=F32).astype(BF16)
                     ).reshape(c // (2 * size), 1, size, c) for p in probs}
        inv = {p: jnp.concatenate([halves[p][:, 0:1], moved[p]] if p[1] == 0 else [moved[p], halves[p][:, 1:2]],
                                  axis=1).reshape(c, c) for p in probs}
    rhs = {p: jnp.concatenate([col(p, 1) * v[p], col(p, 3) * k[p]], axis=1).astype(BF16) for p in probs}
    sol = {p: jnp.dot(inv[p], rhs[p], preferred_element_type=F32) for p in probs}
    qkd = {p: (kq[p][c:] * jnp.where(incl[p[1]], e[p], 0.0)).astype(BF16) for p in probs}
    with_state = {p: jnp.concatenate([sol[p][:, DN_DV:].astype(BF16), (q[p] * col(p, 2)).astype(BF16)], axis=0)
                  for p in probs}
    with_v_new = {p: jnp.concatenate([qkd[p], (k[p] * col(p, 4)).T.astype(BF16)], axis=0) for p in probs}

    state = {ch: state_ref[ch] for ch in chains}
    for step in range(nblk):
        cur = {(bi, d, hd): (bi, d, step if d == 0 else nblk - 1 - step, hd) for bi, d, hd in chains}
        from_state = {ch: jnp.dot(with_state[cur[ch]], state[ch].astype(BF16), preferred_element_type=F32)
                      for ch in chains}
        v_new = {ch: (sol[cur[ch]][:, :DN_DV] - from_state[ch][:c]).astype(BF16) for ch in chains}
        from_v_new = {ch: jnp.dot(with_v_new[cur[ch]], v_new[ch], preferred_element_type=F32)
                      for ch in chains}
        new_state = {}
        for ch in chains:
            bi, d, blk, hd = cur[ch]
            o_refs[d][bi, rows(blk), pl.ds(hd * DN_DV, DN_DV)] = from_state[ch][c:] + from_v_new[ch][:c]
            j = 5 * nh2 + d * DN_HEADS + hd
            etot = pk[(bi, d, blk)][0:1, j:j + 1]
            new_state[ch] = state[ch] * etot + from_v_new[ch][c:]
        state = new_state
    for ch in chains:
        state_ref[ch] = state[ch]

    @pl.when(n == last)
    def _():
        sout_ref[...] = state_ref[...]


def _deltanet(qkv, pack, s0, *, batch, seq_len):
    tokens = qkv.shape[0]
    step = min(DN_STEP_TOKENS, seq_len)
    nb = seq_len // step
    nbatch = DN_BATCH_PER_STEP if batch % DN_BATCH_PER_STEP == 0 else 1
    width = 2 * DN_QK + DN_V
    fwd = lambda b, n: (b, n, 0)
    bwd = lambda b, n: (b, nb - 1 - n, 0)
    state_block = (nbatch, 2, DN_HEADS, DN_DK, DN_DV)
    state_map = lambda b, n: (b, 0, 0, 0, 0)
    masks = _substitution_masks()
    qkv3 = qkv.reshape(batch, seq_len, width)
    pack3 = pack.reshape(batch, seq_len, PACK_W)
    in_specs = [pl.BlockSpec((nbatch, step, width), fwd), pl.BlockSpec((nbatch, step, width), bwd),
                pl.BlockSpec((nbatch, step, PACK_W), fwd), pl.BlockSpec((nbatch, step, PACK_W), bwd),
                _const_spec(masks)]
    args = [qkv3, qkv3, pack3, pack3, masks]
    if s0 is not None:
        in_specs.append(pl.BlockSpec(state_block, state_map))
        args.append(s0)
    o_f, o_b, states = pl.pallas_call(
        functools.partial(_deltanet_kernel, has_s0=s0 is not None, nbatch=nbatch, same_tokens=nb == 1),
        out_shape=(jax.ShapeDtypeStruct((batch, seq_len, DN_V), F32),
                   jax.ShapeDtypeStruct((batch, seq_len, DN_V), F32),
                   jax.ShapeDtypeStruct((batch, 2, DN_HEADS, DN_DK, DN_DV), F32)),
        grid=(batch // nbatch, nb),
        in_specs=in_specs,
        out_specs=(pl.BlockSpec((nbatch, step, DN_V), fwd), pl.BlockSpec((nbatch, step, DN_V), bwd),
                   pl.BlockSpec(state_block, state_map)),
        scratch_shapes=[pltpu.VMEM(state_block, F32)],
        compiler_params=pltpu.CompilerParams(dimension_semantics=("parallel", "arbitrary"),
                                             vmem_limit_bytes=VMEM_LIMIT_BYTES),
        name="deltanet",
    )(*args)
    return o_f.reshape(tokens, DN_V), o_b.reshape(tokens, DN_V), states


def _mixer_out_kernel(x_ref, of_ref, ob_ref, hc_ref, cg_ref, mod_ref, gpre_ref, gpost_ref, dnn_ref,
                      wz_ref, wbg_ref, wmg_ref, wdn_ref, wcf_ref, wsc_ref, wo_ref, scconv_ref,
                      out_ref, min_ref, *, seg, stride):
    tm = TOKENS_PER_TILE
    x = x_ref[...].reshape(tm, D_MODEL)
    hb = _modulated_rmsnorm(x, gpre_ref[...], mod_ref[0, 0:1, :], mod_ref[0, 1:2, :]).astype(BF16)

    o = of_ref[...].reshape(tm, DN_V) + ob_ref[...].reshape(tm, DN_V)
    z = jnp.dot(hb, wz_ref[...], preferred_element_type=F32)
    heads = []
    for hd in range(DN_HEADS):
        oh = _rmsnorm(o[:, hd * DN_DV:(hd + 1) * DN_DV], dnn_ref[...])
        heads.append((oh * _silu(z[:, hd * DN_DV:(hd + 1) * DN_DV])).astype(BF16))
    o_gated = jnp.concatenate(heads, axis=1)

    cg_pad = _pad_rows(cg_ref[...].reshape(tm, SC_WIDTH))
    bg = jnp.dot(hb, wbg_ref[...], preferred_element_type=F32)
    w_sc = scconv_ref[...]
    sc_parts = [_conv3_block(cg_pad, w_sc, seg, stride, r0, ROW_BLOCK) * bg[r0:r0 + ROW_BLOCK]
                for r0 in range(0, tm, ROW_BLOCK)]
    sc_in = jnp.concatenate(sc_parts, axis=0).astype(BF16)
    hc = hc_ref[...].reshape(tm, CF_WIDTH).astype(BF16)

    cb = 256
    for c0 in range(0, D_MODEL, cb):
        cols = slice(c0, c0 + cb)
        y_dn = jnp.dot(o_gated, wdn_ref[:, cols], preferred_element_type=F32)
        y_cf = jnp.dot(hc, wcf_ref[:, cols], preferred_element_type=F32)
        y_sc = jnp.dot(sc_in, wsc_ref[:, cols], preferred_element_type=F32)
        ga = _sigmoid(jnp.dot(hb, wmg_ref[:, c0:c0 + cb], preferred_element_type=F32))
        gb = _sigmoid(jnp.dot(hb, wmg_ref[:, D_MODEL + c0:D_MODEL + c0 + cb], preferred_element_type=F32))
        gc = _sigmoid(jnp.dot(hb, wmg_ref[:, 2 * D_MODEL + c0:2 * D_MODEL + c0 + cb], preferred_element_type=F32))
        min_ref[:, cols] = (ga * y_dn + gb * y_cf + gc * y_sc).astype(BF16)

    m = jnp.dot(min_ref[...], wo_ref[...], preferred_element_type=F32)
    out = x + mod_ref[0, 2:3, :] * _rmsnorm(m, gpost_ref[...])
    out_ref[...] = out.reshape(out_ref.shape)


def _ffn_kernel(x_ref, mod_ref, gpre_ref, gpost_ref, wup_ref, wdown_ref, conv_ref,
                out_ref, act_ref, *, seg, stride):
    tm = TOKENS_PER_TILE
    x = x_ref[...].reshape(tm, D_MODEL)
    hb = _modulated_rmsnorm(x, gpre_ref[...], mod_ref[0, 3:4, :], mod_ref[0, 4:5, :]).astype(BF16)
    for f0 in range(0, D_FF, FF_BLOCK):
        ua = _pad_rows(jnp.dot(hb, wup_ref[:, f0:f0 + FF_BLOCK], preferred_element_type=F32))
        ub = _pad_rows(jnp.dot(hb, wup_ref[:, D_FF + f0:D_FF + f0 + FF_BLOCK], preferred_element_type=F32))
        wa = conv_ref[:, f0:f0 + FF_BLOCK]
        wb = conv_ref[:, D_FF + f0:D_FF + f0 + FF_BLOCK]
        for r0 in range(0, tm, ROW_BLOCK):
            ca = _conv3_block(ua, wa, seg, stride, r0, ROW_BLOCK)
            cb = _conv3_block(ub, wb, seg, stride, r0, ROW_BLOCK)
            act_ref[pl.ds(r0, ROW_BLOCK), pl.ds(f0, FF_BLOCK)] = (_silu(ca) * cb).astype(BF16)
    y = jnp.dot(act_ref[...], wdown_ref[...], preferred_element_type=F32)
    out = x + mod_ref[0, 5:6, :] * _rmsnorm(y, gpost_ref[...])
    out_ref[...] = out.reshape(out_ref.shape)


def _tile_view(a2d, on_grid, batch):
    width = a2d.shape[1]
    if on_grid:
        return a2d.reshape(batch, a2d.shape[0] // (batch * GRID_W), GRID_W, width)
    return a2d.reshape(a2d.shape[0] // TOKENS_PER_TILE, TOKENS_PER_TILE // COL_TILE_W, COL_TILE_W, width)


def _tile_spec(width, on_grid):
    block = (1, TOKENS_PER_TILE // COL_TILE_W, COL_TILE_W, width)
    if on_grid:
        per_batch = GRID_W // COL_TILE_W
        return pl.BlockSpec(block, lambda i: (i // per_batch, 0, i % per_batch, 0))
    return pl.BlockSpec(block, lambda i: (i, 0, 0, 0))


def _mod_spec(mod, on_grid):
    if on_grid:
        per_batch = GRID_W // COL_TILE_W
        return pl.BlockSpec((1, 6, D_MODEL), lambda i: (i // per_batch, 0, 0))
    return pl.BlockSpec((1, 6, D_MODEL), lambda i: (0, 0, 0))


def _v_conv_geometry(on_grid):
    return (TOKENS_PER_TILE, COL_TILE_W) if on_grid else (256, 1)


def _mixer_out(x2d, o_f, o_b, hc, cgxh, mod, p, *, on_grid, batch):
    tokens = x2d.shape[0]
    seg, stride = _v_conv_geometry(on_grid)
    consts = [p["g_pre_mix"], p["g_post_mix"], p["dn_norm_g"], p["w_z"], p["w_bg"], p["w_mgate"],
              p["w_dn_out"], p["w_cf_out"], p["w_sc_out"], p["w_o"], p["sc_conv"]]
    tiled = [x2d, o_f, o_b, hc, cgxh]
    out = pl.pallas_call(
        functools.partial(_mixer_out_kernel, seg=seg, stride=stride),
        out_shape=jax.ShapeDtypeStruct(_tile_view(x2d, on_grid, batch).shape, F32),
        grid=(tokens // TOKENS_PER_TILE,),
        in_specs=[_tile_spec(a.shape[1], on_grid) for a in tiled] + [_mod_spec(mod, on_grid)]
                 + [_const_spec(c) for c in consts],
        out_specs=_tile_spec(D_MODEL, on_grid),
        scratch_shapes=[pltpu.VMEM((TOKENS_PER_TILE, D_MODEL), BF16)],
        compiler_params=pltpu.CompilerParams(dimension_semantics=("parallel",),
                                             vmem_limit_bytes=VMEM_LIMIT_BYTES),
        name="mixer_out",
    )(*[_tile_view(a, on_grid, batch) for a in tiled], mod, *map(_const_arg, consts))
    return out.reshape(tokens, D_MODEL)


def _ffn(x2d, mod, p, *, on_grid, batch):
    tokens = x2d.shape[0]
    seg, stride = _v_conv_geometry(on_grid)
    consts = [p["g_pre_ffn"], p["g_post_ffn"], p["w_ffn_up"], p["w_ffn_down"], p["ffn_conv"]]
    out = pl.pallas_call(
        functools.partial(_ffn_kernel, seg=seg, stride=stride),
        out_shape=jax.ShapeDtypeStruct(_tile_view(x2d, on_grid, batch).shape, F32),
        grid=(tokens // TOKENS_PER_TILE,),
        in_specs=[_tile_spec(D_MODEL, on_grid), _mod_spec(mod, on_grid)] + [_const_spec(c) for c in consts],
        out_specs=_tile_spec(D_MODEL, on_grid),
        scratch_shapes=[pltpu.VMEM((TOKENS_PER_TILE, D_FF), BF16)],
        compiler_params=pltpu.CompilerParams(dimension_semantics=("parallel",),
                                             vmem_limit_bytes=VMEM_LIMIT_BYTES),
        name="ffn",
    )(_tile_view(x2d, on_grid, batch), mod, *map(_const_arg, consts))
    return out.reshape(tokens, D_MODEL)


_W_IN_PARTS = (("w_qkv", _OFF_Q, _OFF_Z), ("w_z", _OFF_Z, _OFF_A), ("w_dngate", _OFF_A, _OFF_CF),
               ("w_cf", _OFF_CF, _OFF_SC), ("w_bg", _OFF_SC, _OFF_SC + SC_WIDTH),
               ("w_sc", _OFF_SC + SC_WIDTH, _OFF_GATE), ("w_mgate", _OFF_GATE, _OFF_GATE + 3 * D_MODEL))


def _split_w_in_kernel(wt_ref, *out_refs, parts):
    wt = wt_ref[0]
    for out_ref, (_, lo, hi) in zip(out_refs, parts):
        out_ref[0] = wt[lo:hi, :].T.astype(BF16)


def _split_w_in(w_in):
    depth, rows, cols = w_in.shape
    rb = 256
    parts = [part for part in _W_IN_PARTS if (part[2] - part[1]) % LANES == 0]
    assert all(lo % SUBLANES == 0 for _, lo, _ in parts)
    widths = [hi - lo for _, lo, hi in parts]
    outs = pl.pallas_call(
        functools.partial(_split_w_in_kernel, parts=parts),
        out_shape=tuple(jax.ShapeDtypeStruct((depth, rows, width), BF16) for width in widths),
        grid=(depth, rows // rb),
        in_specs=[pl.BlockSpec((1, cols, rb), lambda l, r: (l, 0, r))],
        out_specs=tuple(pl.BlockSpec((1, rb, width), lambda l, r: (l, r, 0)) for width in widths),
        compiler_params=pltpu.CompilerParams(dimension_semantics=("parallel", "parallel"),
                                             vmem_limit_bytes=VMEM_LIMIT_BYTES),
        name="split_w_in",
    )(jnp.swapaxes(w_in, 1, 2))
    split = {name: out for (name, _, _), out in zip(parts, outs)}
    for name, lo, hi in _W_IN_PARTS:
        if name not in split:
            split[name] = w_in[:, :, lo:hi].astype(BF16)
    return split


def _matrix_weights(w_in, w_dn_out, w_cf_out, w_sc_out, w_o, w_ffn_up, w_ffn_down):
    stacked = _split_w_in(w_in)
    stacked.update(w_dn_out=w_dn_out.astype(BF16), w_cf_out=w_cf_out.astype(BF16), w_sc_out=w_sc_out.astype(BF16),
                   w_o=w_o.astype(BF16), w_ffn_up=w_ffn_up.astype(BF16), w_ffn_down=w_ffn_down.astype(BF16))
    return stacked


def _layer_params(l, matrices, g_pre_mix, g_post_mix, g_pre_ffn, g_post_ffn, dn_conv, dn_a_log, dn_dt_bias,
                  dn_norm_g, cf_conv, cf_ln_g, cf_ln_b, sc_conv, ffn_conv):
    row = lambda v: v.reshape(1, -1).astype(F32)
    return {
        **{name: (stacked, l) for name, stacked in matrices.items()},
        "g_pre_mix": row(g_pre_mix[l]), "g_post_mix": row(g_post_mix[l]),
        "g_pre_ffn": row(g_pre_ffn[l]), "g_post_ffn": row(g_post_ffn[l]),
        "dn_conv": dn_conv[l], "dn_a_log": row(dn_a_log[l]), "dn_dt_bias": row(dn_dt_bias[l]),
        "dn_norm_g": row(dn_norm_g[l]), "cf_conv": cf_conv[l], "cf_ln_g": row(cf_ln_g[l]),
        "cf_ln_b": row(cf_ln_b[l]), "sc_conv": sc_conv[l], "ffn_conv": ffn_conv[l],
    }


def _trunk_layer(x2d, mod, p, s0, *, batch, seq_len, on_grid):
    seg_h = GRID_W if on_grid else seq_len
    qkv, pack, hc, cgxh = _mixer_in(x2d, mod, p, seq_len=seq_len, seg=seg_h)
    o_f, o_b, states = _deltanet(qkv, pack, s0, batch=batch, seq_len=seq_len)
    x2d = _mixer_out(x2d, o_f, o_b, hc, cgxh, mod, p, on_grid=on_grid, batch=batch)
    x2d = _ffn(x2d, mod, p, on_grid=on_grid, batch=batch)
    return x2d, states


def kernel(x_prompt, x_sample, state_dn, c, c_ctx, w_mod, b_mod, g_pre_mix, g_post_mix, g_pre_ffn, g_post_ffn,
           w_in, dn_conv, dn_a_log, dn_dt_bias, dn_norm_g, w_dn_out, cf_conv, cf_ln_g, cf_ln_b, w_cf_out,
           sc_conv, w_sc_out, w_o, w_ffn_up, ffn_conv, w_ffn_down):
    batch, seq, _ = x_prompt.shape
    dec_batch, dec_seq, _ = x_sample.shape
    assert dec_batch + 1 <= 8
    c_all = jnp.concatenate([c, c_ctx[None, :], jnp.zeros((8 - dec_batch - 1, D_MODEL), F32)], axis=0)
    mods = _modulation(c_all, w_mod, b_mod)
    xp = x_prompt.reshape(batch * seq, D_MODEL)
    xs = x_sample.reshape(dec_batch * dec_seq, D_MODEL)
    ctx_states = []
    matrices = _matrix_weights(w_in, w_dn_out, w_cf_out, w_sc_out, w_o, w_ffn_up, w_ffn_down)
    for l in range(DEPTH):
        p = _layer_params(l, matrices, g_pre_mix, g_post_mix, g_pre_ffn, g_post_ffn, dn_conv, dn_a_log,
                          dn_dt_bias, dn_norm_g, cf_conv, cf_ln_g, cf_ln_b, sc_conv, ffn_conv)
        mod_lat = mods[l, :dec_batch].reshape(dec_batch, 6, D_MODEL)
        mod_ctx = mods[l, dec_batch:dec_batch + 1].reshape(1, 6, D_MODEL)
        xp, st = _trunk_layer(xp, mod_ctx, p, None, batch=batch, seq_len=seq, on_grid=False)
        ctx_states.append(st)
        xs, _ = _trunk_layer(xs, mod_lat, p, state_dn[:, l].astype(F32), batch=dec_batch, seq_len=dec_seq,
                             on_grid=True)
    new_state = jnp.stack(ctx_states, axis=1).astype(x_prompt.dtype)
    return (xp.reshape(batch, seq, D_MODEL), xs.reshape(dec_batch, dec_seq, D_MODEL), new_state)
```

```python
import functools

import numpy as np
import jax
import jax.numpy as jnp
from jax import lax
from jax.experimental import pallas as pl
from jax.experimental.pallas import tpu as pltpu

F32 = jnp.float32
BF16 = jnp.bfloat16

D_MODEL = 1024
DEPTH = 2
GRID_W = 64
DN_HEADS = 4
DN_DK = 128
DN_DV = 128
DN_BLOCK = 128
DN_STEP_TOKENS = 256
DN_BATCH_PER_STEP = 2
DN_QK = DN_HEADS * DN_DK
DN_V = DN_HEADS * DN_DV
CF_WIDTH = 512
CF_CONV_W = 31
SC_WIDTH = 512
D_FF = 2816
EPS = 1e-6

_OFF_Q = 0
_OFF_Z = 2 * DN_QK + DN_V
_OFF_A = _OFF_Z + DN_V
_OFF_CF = _OFF_A + 4 * DN_HEADS
_OFF_SC = _OFF_CF + 2 * CF_WIDTH
_OFF_GATE = _OFF_SC + 3 * SC_WIDTH

SUBLANES = 8
BF16_SUBLANES = 16
LANES = 128
TOKENS_PER_TILE = 512
COL_TILE_W = 8
CONV_GAP = 16
ROW_BLOCK = 64
FF_BLOCK = 256
MERGE_COL_BLOCK = 256
MOD_COL_BLOCK = 1024
SPLIT_ROW_BLOCK = 256
PACK_W = 128
VMEM_LIMIT_BYTES = 56 * 1024 * 1024


def _sigmoid(x):
    return 0.5 * jnp.tanh(0.5 * x) + 0.5


def _silu(x):
    h = 0.5 * x
    return h + h * jnp.tanh(h)


def _softplus(x):
    return jnp.maximum(x, 0.0) + jnp.log1p(jnp.exp(-jnp.abs(x)))


def _mm(a, b):
    return jnp.dot(a.astype(BF16), b.astype(BF16), preferred_element_type=F32)


def _mm_nt(a, b):
    return lax.dot_general(a.astype(BF16), b.astype(BF16), (((1,), (1,)), ((), ())),
                           preferred_element_type=F32)


def _modulated_rmsnorm(x, gain, shift, scale):
    y = x * lax.rsqrt(jnp.mean(x * x, axis=-1, keepdims=True) + EPS)
    return y * (gain * (1.0 + scale)) + shift


def _rmsnorm(x, gain):
    return x * lax.rsqrt(jnp.mean(x * x, axis=-1, keepdims=True) + EPS) * gain


def _store_segments(pad_ref, val, seg, gap):
    rows, width = val.shape
    zeros = jnp.zeros((gap, width), F32)
    pad_ref[pl.ds(0, gap), pl.ds(0, width)] = zeros
    for s in range(rows // seg):
        base = gap + s * (seg + gap)
        pad_ref[pl.ds(base, seg), pl.ds(0, width)] = val[s * seg:(s + 1) * seg]
        pad_ref[pl.ds(base + seg, gap), pl.ds(0, width)] = zeros


def _conv_block(pad_ref, w_ref, base, rows, c0, cols, w_c0):
    taps = w_ref.shape[0]
    offs = [k - taps // 2 for k in range(taps)]
    halo = SUBLANES * (-(-max(offs) // SUBLANES))
    n = rows + 2 * halo
    win = pad_ref[pl.ds(base - halo, n), pl.ds(c0, cols)]
    by_residue = {}
    for k, off in enumerate(offs):
        by_residue.setdefault(off % SUBLANES, []).append((k, off // SUBLANES))
    acc = None
    for b in sorted(by_residue):
        rolled = win if b == 0 else pltpu.roll(win, n - b, axis=0)
        for k, a in by_residue[b]:
            start = halo + SUBLANES * a
            term = rolled[start:start + rows] * w_ref[pl.ds(k, 1), pl.ds(w_c0, cols)]
            acc = term if acc is None else acc + term
    return acc


def _pad_rows(u):
    zeros = jnp.zeros((SUBLANES, u.shape[1]), u.dtype)
    return jnp.concatenate([zeros, u, zeros], axis=0)


def _conv3_block(u_pad, w, seg, stride, r0, rows):
    cur = u_pad[r0 + SUBLANES:r0 + SUBLANES + rows]
    if stride == SUBLANES:
        prev = u_pad[r0:r0 + rows]
        nxt = u_pad[r0 + 2 * SUBLANES:r0 + 2 * SUBLANES + rows]
    else:
        assert stride == 1 and seg % rows == 0
        n = rows + 2 * SUBLANES
        win = u_pad[r0:r0 + n]
        prev = pltpu.roll(win, 1, axis=0)[SUBLANES:SUBLANES + rows]
        nxt = pltpu.roll(win, n - 1, axis=0)[SUBLANES:SUBLANES + rows]
        sub = lax.broadcasted_iota(jnp.int32, (SUBLANES, u_pad.shape[1]), 0)
        if r0 % seg == 0:
            prev = jnp.concatenate([jnp.where(sub == 0, 0.0, prev[:SUBLANES]), prev[SUBLANES:]], axis=0)
        if (r0 + rows) % seg == 0:
            nxt = jnp.concatenate([nxt[:-SUBLANES], jnp.where(sub == SUBLANES - 1, 0.0, nxt[-SUBLANES:])], axis=0)
    return prev * w[0:1] + cur * w[1:2] + nxt * w[2:3]


def _mod_kernel(c_ref, w_ref, b_ref, o_ref):
    s = _silu(c_ref[...])
    o_ref[0] = _mm(s, w_ref[0]) + b_ref[0]


def _modulation(c_all, w_mod, b_mod):
    tn = MOD_COL_BLOCK
    n_cols = 6 * D_MODEL
    return pl.pallas_call(
        _mod_kernel,
        out_shape=jax.ShapeDtypeStruct((DEPTH, SUBLANES, n_cols), F32),
        grid=(DEPTH, n_cols // tn),
        in_specs=[
            pl.BlockSpec((SUBLANES, D_MODEL), lambda l, j: (0, 0)),
            pl.BlockSpec((1, D_MODEL, tn), lambda l, j: (l, 0, j)),
            pl.BlockSpec((1, 1, tn), lambda l, j: (l, 0, j)),
        ],
        out_specs=pl.BlockSpec((1, SUBLANES, tn), lambda l, j: (l, 0, j)),
        compiler_params=pltpu.CompilerParams(dimension_semantics=("parallel", "parallel"),
                                             vmem_limit_bytes=VMEM_LIMIT_BYTES),
        name="modulation",
    )(c_all, w_mod, b_mod.reshape(DEPTH, 1, n_cols))


def _mixer_in_kernel(x_ref, mod_ref, gpre_ref, wqkv_ref, wgate_ref, wcf_ref, wsc_ref,
                     dnconv_ref, cfconv_ref, lng_ref, lnb_ref, alog_ref, dtb_ref,
                     trif_ref, trib_ref, ones_ref,
                     qkv_out, pack_out, hc_out, cgxh_out,
                     pad_ref, *, seg):
    tm = x_ref.shape[0]
    nseg = tm // seg
    gap = CONV_GAP
    hb = _modulated_rmsnorm(x_ref[...], gpre_ref[...], mod_ref[0, 0:1, :], mod_ref[0, 1:2, :]).astype(BF16)

    for grp in range(3):
        proj = jnp.dot(hb, wqkv_ref[:, grp * DN_QK:(grp + 1) * DN_QK], preferred_element_type=F32)
        _store_segments(pad_ref, proj, seg, gap)
        for s in range(nseg):
            for r0 in range(0, seg, ROW_BLOCK):
                base = gap + s * (seg + gap) + r0
                rows = pl.ds(s * seg + r0, ROW_BLOCK)
                for hd in range(DN_HEADS):
                    col = grp * DN_QK + hd * DN_DK
                    a = _silu(_conv_block(pad_ref, dnconv_ref, base, ROW_BLOCK, hd * DN_DK, DN_DK, col))
                    if grp < 2:
                        inv_norm = lax.rsqrt(jnp.sum(a * a, axis=-1, keepdims=True) + EPS)
                        a = a * (inv_norm * DN_DK ** -0.5 if grp == 0 else inv_norm)
                    qkv_out[rows, pl.ds(col, DN_DK)] = a

    nh2 = 2 * DN_HEADS
    ab = jnp.dot(hb, wgate_ref[...], preferred_element_type=F32)
    g = -jnp.exp(alog_ref[...]) * _softplus(ab[:, 0:nh2] + dtb_ref[...])
    beta = _sigmoid(ab[:, nh2:2 * nh2])
    g1 = g.astype(BF16)
    r1 = g - g1.astype(F32)
    g2 = r1.astype(BF16)
    g3 = (r1 - g2.astype(F32)).astype(BF16)
    pieces = jnp.concatenate([g1, g2, g3], axis=1)

    def summed(mat_ref):
        c = jnp.dot(mat_ref[...], pieces, preferred_element_type=F32)
        return c[:, 0:nh2] + c[:, nh2:2 * nh2] + c[:, 2 * nh2:3 * nh2]

    lane = lax.broadcasted_iota(jnp.int32, (tm, nh2), 1)
    gam = jnp.where(lane < DN_HEADS, summed(trif_ref), summed(trib_ref))
    tot = summed(ones_ref)
    expg = jnp.exp(gam)
    pack_out[...] = jnp.concatenate(
        [gam, beta, expg, beta * expg, jnp.exp(tot - gam), jnp.exp(tot),
         jnp.zeros((tm, PACK_W - 6 * nh2), F32)], axis=1)

    pc = jnp.dot(hb, wcf_ref[...], preferred_element_type=F32)
    glu = pc[:, :CF_WIDTH] * _sigmoid(pc[:, CF_WIDTH:])
    _store_segments(pad_ref, glu, seg, gap)

    ps = jnp.dot(hb, wsc_ref[...], preferred_element_type=F32)
    cgxh_out[...] = ps[:, :SC_WIDTH] * ps[:, SC_WIDTH:]

    for s in range(nseg):
        for r0 in range(0, seg, ROW_BLOCK):
            base = gap + s * (seg + gap) + r0
            c = jnp.concatenate([_conv_block(pad_ref, cfconv_ref, base, ROW_BLOCK, c0, LANES, c0)
                                 for c0 in range(0, CF_WIDTH, LANES)], axis=1)
            mu = jnp.mean(c, axis=-1, keepdims=True)
            cc = c - mu
            y = cc * lax.rsqrt(jnp.mean(cc * cc, axis=-1, keepdims=True) + EPS)
            y = y * lng_ref[...] + lnb_ref[...]
            hc_out[pl.ds(s * seg + r0, ROW_BLOCK), :] = _silu(y)


def _chunk_matrices(tm):
    idx = np.arange(tm)
    same = (idx[:, None] // DN_BLOCK) == (idx[None, :] // DN_BLOCK)
    lower = same & (idx[:, None] >= idx[None, :])
    upper = same & (idx[:, None] <= idx[None, :])
    as_bf16 = lambda m: jnp.asarray(m.astype(np.float32), dtype=BF16)
    return as_bf16(lower), as_bf16(upper), as_bf16(same)


def _const_spec(operand):
    if isinstance(operand, tuple):
        stacked, layer = operand
        rest = stacked.ndim - 1
        return pl.BlockSpec((None,) + stacked.shape[1:], lambda *_: (layer,) + (0,) * rest,
                            pipeline_mode=pl.Buffered(1))
    nd = operand.ndim
    return pl.BlockSpec(operand.shape, lambda *_: (0,) * nd, pipeline_mode=pl.Buffered(1))


def _const_arg(operand):
    return operand[0] if isinstance(operand, tuple) else operand


def _mixer_in(x2d, mod, p, *, seq_len, seg):
    tokens = x2d.shape[0]
    tm = TOKENS_PER_TILE
    tiles_per_mod = max(seq_len // tm, 1) if mod.shape[0] > 1 else None
    mod_map = (lambda i: (i // tiles_per_mod, 0, 0)) if tiles_per_mod else (lambda i: (0, 0, 0))
    trif, trib, ones = _chunk_matrices(tm)
    pad_rows = CONV_GAP + (tm // seg) * (seg + CONV_GAP)
    row = lambda i: (i, 0)
    consts = [p["g_pre_mix"], p["w_qkv"], p["w_dngate"], p["w_cf"], p["w_sc"], p["dn_conv"], p["cf_conv"],
              p["cf_ln_g"], p["cf_ln_b"], p["dn_a_log"], p["dn_dt_bias"], trif, trib, ones]
    return pl.pallas_call(
        functools.partial(_mixer_in_kernel, seg=seg),
        out_shape=(jax.ShapeDtypeStruct((tokens, 2 * DN_QK + DN_V), F32),
                   jax.ShapeDtypeStruct((tokens, PACK_W), F32),
                   jax.ShapeDtypeStruct((tokens, CF_WIDTH), F32),
                   jax.ShapeDtypeStruct((tokens, SC_WIDTH), F32)),
        grid=(tokens // tm,),
        in_specs=[pl.BlockSpec((tm, D_MODEL), row), pl.BlockSpec((1, 6, D_MODEL), mod_map)]
                 + [_const_spec(c) for c in consts],
        out_specs=(pl.BlockSpec((tm, 2 * DN_QK + DN_V), row), pl.BlockSpec((tm, PACK_W), row),
                   pl.BlockSpec((tm, CF_WIDTH), row), pl.BlockSpec((tm, SC_WIDTH), row)),
        scratch_shapes=[pltpu.VMEM((pad_rows, DN_QK), F32)],
        compiler_params=pltpu.CompilerParams(dimension_semantics=("parallel",),
                                             vmem_limit_bytes=VMEM_LIMIT_BYTES),
        name="mixer_in",
    )(x2d, mod, *map(_const_arg, consts))


def _substitution_masks():
    n = DN_BLOCK
    i = np.arange(n)[:, None]
    j = np.arange(n)[None, :]
    out = []
    for strict in (i > j, i < j):
        levels = [strict & (i // 2 == j // 2)]
        size = 2
        while size < n:
            levels.append(strict & (i // (2 * size) == j // (2 * size)) & (i // size != j // size))
            size *= 2
        out.append(np.stack(levels))
    return jnp.asarray(np.stack(out).astype(np.float32), dtype=BF16)


def _deltanet_kernel(qf_ref, qb_ref, pf_ref, pb_ref, masks_ref, *rest, has_s0, nbatch, same_tokens):
    if has_s0:
        s0_ref, of_ref, ob_ref, sout_ref, state_ref = rest
    else:
        of_ref, ob_ref, sout_ref, state_ref = rest
    n = pl.program_id(1)
    last = pl.num_programs(1) - 1
    c = DN_BLOCK
    nh2 = 2 * DN_HEADS
    n_levels = masks_ref.shape[1]

    @pl.when(n == 0)
    def _():
        if has_s0:
            state_ref[...] = s0_ref[...]
        else:
            state_ref[...] = jnp.zeros(state_ref.shape, F32)

    ii = lax.broadcasted_iota(jnp.int32, (c, c), 0)
    jj = lax.broadcasted_iota(jnp.int32, (c, c), 1)
    eye = (ii == jj).astype(F32).astype(BF16)
    incl = ((ii >= jj), (ii <= jj))
    strict = ((ii > jj), (ii < jj))
    q_refs, p_refs, o_refs = (qf_ref, qb_ref), (pf_ref, pb_ref), (of_ref, ob_ref)

    nblk = qf_ref.shape[1] // c
    chains = [(bi, d, hd) for bi in range(nbatch) for d in (0, 1) for hd in range(DN_HEADS)]
    probs = [(bi, d, blk, hd) for bi, d, hd in chains for blk in range(nblk)]
    rows = lambda blk: pl.ds(blk * c, c)
    pk = {(bi, d, blk): p_refs[d][bi, rows(blk), :]
          for bi in range(nbatch) for d in (0, 1) for blk in range(nblk)}
    pk_t = {key: val.T for key, val in pk.items()}

    def col(p, grp):
        bi, d, blk, hd = p
        j = grp * nh2 + d * DN_HEADS + hd
        return pk[(bi, d, blk)][:, j:j + 1]

    def head(p, off):
        bi, d, blk, hd = p
        return q_refs[d][bi, rows(blk), pl.ds(off + hd * DN_DK, DN_DK)]

    q = {p: head(p, 0) for p in probs}
    k = {p: head(p, DN_QK) for p in probs}
    v = {p: head(p, 2 * DN_QK) for p in probs}
    kb = {p: k[p].astype(BF16) for p in probs}
    kq = {}
    for p in probs:
        bi, d, blk, hd = p
        twin = (bi, 0, blk, hd)
        if same_tokens and d == 1:
            kq[p] = kq[twin]
        else:
            kq[p] = _mm_nt(jnp.concatenate([kb[p], q[p].astype(BF16)], axis=0), kb[p])
    kk = {p: kq[p][:c] for p in probs}
    e = {}
    for p in probs:
        bi, d, blk, hd = p
        gam_r = pk_t[(bi, d, blk)][d * DN_HEADS + hd:d * DN_HEADS + hd + 1, :]
        e[p] = jnp.exp(jnp.where(incl[d], col(p, 0) - gam_r, 0.0))
    a = {p: jnp.where(strict[p[1]], col(p, 1) * kk[p] * e[p], 0.0).astype(BF16) for p in probs}
    inv = {p: eye - a[p] * masks_ref[p[1], 0] for p in probs}
    for lvl in range(1, n_levels):
        size = 2 ** lvl
        off = {p: a[p] * masks_ref[p[1], lvl] for p in probs}
        if size % BF16_SUBLANES:
            t = {p: jnp.dot(inv[p], off[p], preferred_element_type=F32).astype(BF16) for p in probs}
            inv = {p: inv[p] - jnp.dot(t[p], inv[p], preferred_element_type=F32).astype(BF16) for p in probs}
            continue
        halves = {p: inv[p].reshape(c // (2 * size), 2, size, c) for p in probs}
        moving = {p: halves[p][:, 1 - p[1]].reshape(c // 2, c) for p in probs}
        t = {p: jnp.dot(moving[p], off[p], preferred_element_type=F32).astype(BF16) for p in probs}
        moved = {p: (moving[p] - jnp.dot(t[p], inv[p], preferred_element_type=F32).astype(BF16)
                     ).reshape(c // (2 * size), 1, size, c) for p in probs}
        inv = {p: jnp.concatenate([halves[p][:, 0:1], moved[p]] if p[1] == 0 else [moved[p], halves[p][:, 1:2]],
                                  axis=1).reshape(c, c) for p in probs}
    rhs = {p: jnp.concatenate([col(p, 1) * v[p], col(p, 3) * k[p]], axis=1).astype(BF16) for p in probs}
    sol = {p: jnp.dot(inv[p], rhs[p], preferred_element_type=F32) for p in probs}
    qkd = {p: (kq[p][c:] * jnp.where(incl[p[1]], e[p], 0.0)).astype(BF16) for p in probs}
    with_state = {p: jnp.concatenate([sol[p][:, DN_DV:].astype(BF16), (q[p] * col(p, 2)).astype(BF16)], axis=0)
                  for p in probs}
    with_v_new = {p: jnp.concatenate([qkd[p], (k[p] * col(p, 4)).T.astype(BF16)], axis=0) for p in probs}

    state = {ch: state_ref[ch] for ch in chains}
    for step in range(nblk):
        cur = {(bi, d, hd): (bi, d, step if d == 0 else nblk - 1 - step, hd) for bi, d, hd in chains}
        from_state = {ch: jnp.dot(with_state[cur[ch]], state[ch].astype(BF16), preferred_element_type=F32)
                      for ch in chains}
        v_new = {ch: (sol[cur[ch]][:, :DN_DV] - from_state[ch][:c]).astype(BF16) for ch in chains}
        from_v_new = {ch: jnp.dot(with_v_new[cur[ch]], v_new[ch], preferred_element_type=F32)
                      for ch in chains}
        new_state = {}
        for ch in chains:
            bi, d, blk, hd = cur[ch]
            o_refs[d][bi, rows(blk), pl.ds(hd * DN_DV, DN_DV)] = from_state[ch][c:] + from_v_new[ch][:c]
            j = 5 * nh2 + d * DN_HEADS + hd
            etot = pk[(bi, d, blk)][0:1, j:j + 1]
            new_state[ch] = state[ch] * etot + from_v_new[ch][c:]
        state = new_state
    for ch in chains:
        state_ref[ch] = state[ch]

    @pl.when(n == last)
    def _():
        sout_ref[...] = state_ref[...]


def _deltanet(qkv, pack, s0, *, batch, seq_len):
    tokens = qkv.shape[0]
    step = min(DN_STEP_TOKENS, seq_len)
    nb = seq_len // step
    nbatch = DN_BATCH_PER_STEP if batch % DN_BATCH_PER_STEP == 0 else 1
    width = 2 * DN_QK + DN_V
    fwd = lambda b, n: (b, n, 0)
    bwd = lambda b, n: (b, nb - 1 - n, 0)
    state_block = (nbatch, 2, DN_HEADS, DN_DK, DN_DV)
    state_map = lambda b, n: (b, 0, 0, 0, 0)
    masks = _substitution_masks()
    qkv3 = qkv.reshape(batch, seq_len, width)
    pack3 = pack.reshape(batch, seq_len, PACK_W)
    in_specs = [pl.BlockSpec((nbatch, step, width), fwd), pl.BlockSpec((nbatch, step, width), bwd),
                pl.BlockSpec((nbatch, step, PACK_W), fwd), pl.BlockSpec((nbatch, step, PACK_W), bwd),
                _const_spec(masks)]
    args = [qkv3, qkv3, pack3, pack3, masks]
    if s0 is not None:
        in_specs.append(pl.BlockSpec(state_block, state_map))
        args.append(s0)
    o_f, o_b, states = pl.pallas_call(
        functools.partial(_deltanet_kernel, has_s0=s0 is not None, nbatch=nbatch, same_tokens=nb == 1),
        out_shape=(jax.ShapeDtypeStruct((batch, seq_len, DN_V), F32),
                   jax.ShapeDtypeStruct((batch, seq_len, DN_V), F32),
                   jax.ShapeDtypeStruct((batch, 2, DN_HEADS, DN_DK, DN_DV), F32)),
        grid=(batch // nbatch, nb),
        in_specs=in_specs,
        out_specs=(pl.BlockSpec((nbatch, step, DN_V), fwd), pl.BlockSpec((nbatch, step, DN_V), bwd),
                   pl.BlockSpec(state_block, state_map)),
        scratch_shapes=[pltpu.VMEM(state_block, F32)],
        compiler_params=pltpu.CompilerParams(dimension_semantics=("parallel", "arbitrary"),
                                             vmem_limit_bytes=VMEM_LIMIT_BYTES),
        name="deltanet",
    )(*args)
    return o_f.reshape(tokens, DN_V), o_b.reshape(tokens, DN_V), states


def _mixer_out_kernel(x_ref, of_ref, ob_ref, hc_ref, cg_ref, mod_ref, gpre_ref, gpost_ref, dnn_ref,
                      wz_ref, wbg_ref, wmg_ref, wdn_ref, wcf_ref, wsc_ref, wo_ref, scconv_ref,
                      out_ref, min_ref, *, seg, stride):
    tm = TOKENS_PER_TILE
    x = x_ref[...].reshape(tm, D_MODEL)
    hb = _modulated_rmsnorm(x, gpre_ref[...], mod_ref[0, 0:1, :], mod_ref[0, 1:2, :]).astype(BF16)

    o = of_ref[...].reshape(tm, DN_V) + ob_ref[...].reshape(tm, DN_V)
    z = jnp.dot(hb, wz_ref[...], preferred_element_type=F32)
    heads = []
    for hd in range(DN_HEADS):
        oh = _rmsnorm(o[:, hd * DN_DV:(hd + 1) * DN_DV], dnn_ref[...])
        heads.append((oh * _silu(z[:, hd * DN_DV:(hd + 1) * DN_DV])).astype(BF16))
    o_gated = jnp.concatenate(heads, axis=1)

    cg_pad = _pad_rows(cg_ref[...].reshape(tm, SC_WIDTH))
    bg = jnp.dot(hb, wbg_ref[...], preferred_element_type=F32)
    w_sc = scconv_ref[...]
    sc_parts = [_conv3_block(cg_pad, w_sc, seg, stride, r0, ROW_BLOCK) * bg[r0:r0 + ROW_BLOCK]
                for r0 in range(0, tm, ROW_BLOCK)]
    sc_in = jnp.concatenate(sc_parts, axis=0).astype(BF16)
    hc = hc_ref[...].reshape(tm, CF_WIDTH).astype(BF16)

    cb = MERGE_COL_BLOCK
    for c0 in range(0, D_MODEL, cb):
        cols = slice(c0, c0 + cb)
        y_dn = jnp.dot(o_gated, wdn_ref[:, cols], preferred_element_type=F32)
        y_cf = jnp.dot(hc, wcf_ref[:, cols], preferred_element_type=F32)
        y_sc = jnp.dot(sc_in, wsc_ref[:, cols], preferred_element_type=F32)
        ga = _sigmoid(jnp.dot(hb, wmg_ref[:, c0:c0 + cb], preferred_element_type=F32))
        gb = _sigmoid(jnp.dot(hb, wmg_ref[:, D_MODEL + c0:D_MODEL + c0 + cb], preferred_element_type=F32))
        gc = _sigmoid(jnp.dot(hb, wmg_ref[:, 2 * D_MODEL + c0:2 * D_MODEL + c0 + cb], preferred_element_type=F32))
        min_ref[:, cols] = (ga * y_dn + gb * y_cf + gc * y_sc).astype(BF16)

    m = jnp.dot(min_ref[...], wo_ref[...], preferred_element_type=F32)
    out = x + mod_ref[0, 2:3, :] * _rmsnorm(m, gpost_ref[...])
    out_ref[...] = out.reshape(out_ref.shape)


def _ffn_kernel(x_ref, mod_ref, gpre_ref, gpost_ref, wup_ref, wdown_ref, conv_ref,
                out_ref, act_ref, *, seg, stride):
    tm = TOKENS_PER_TILE
    x = x_ref[...].reshape(tm, D_MODEL)
    hb = _modulated_rmsnorm(x, gpre_ref[...], mod_ref[0, 3:4, :], mod_ref[0, 4:5, :]).astype(BF16)
    for f0 in range(0, D_FF, FF_BLOCK):
        ua = _pad_rows(jnp.dot(hb, wup_ref[:, f0:f0 + FF_BLOCK], preferred_element_type=F32))
        ub = _pad_rows(jnp.dot(hb, wup_ref[:, D_FF + f0:D_FF + f0 + FF_BLOCK], preferred_element_type=F32))
        wa = conv_ref[:, f0:f0 + FF_BLOCK]
        wb = conv_ref[:, D_FF + f0:D_FF + f0 + FF_BLOCK]
        for r0 in range(0, tm, ROW_BLOCK):
            ca = _conv3_block(ua, wa, seg, stride, r0, ROW_BLOCK)
            cb = _conv3_block(ub, wb, seg, stride, r0, ROW_BLOCK)
            act_ref[pl.ds(r0, ROW_BLOCK), pl.ds(f0, FF_BLOCK)] = (_silu(ca) * cb).astype(BF16)
    y = jnp.dot(act_ref[...], wdown_ref[...], preferred_element_type=F32)
    out = x + mod_ref[0, 5:6, :] * _rmsnorm(y, gpost_ref[...])
    out_ref[...] = out.reshape(out_ref.shape)


def _tile_view(a2d, on_grid, batch):
    width = a2d.shape[1]
    if on_grid:
        return a2d.reshape(batch, a2d.shape[0] // (batch * GRID_W), GRID_W, width)
    return a2d.reshape(a2d.shape[0] // TOKENS_PER_TILE, TOKENS_PER_TILE // COL_TILE_W, COL_TILE_W, width)


def _tile_spec(width, on_grid):
    block = (1, TOKENS_PER_TILE // COL_TILE_W, COL_TILE_W, width)
    if on_grid:
        per_batch = GRID_W // COL_TILE_W
        return pl.BlockSpec(block, lambda i: (i // per_batch, 0, i % per_batch, 0))
    return pl.BlockSpec(block, lambda i: (i, 0, 0, 0))


def _mod_spec(mod, on_grid):
    if on_grid:
        per_batch = GRID_W // COL_TILE_W
        return pl.BlockSpec((1, 6, D_MODEL), lambda i: (i // per_batch, 0, 0))
    return pl.BlockSpec((1, 6, D_MODEL), lambda i: (0, 0, 0))


def _v_conv_geometry(on_grid):
    return (TOKENS_PER_TILE, COL_TILE_W) if on_grid else (256, 1)


def _mixer_out(x2d, o_f, o_b, hc, cgxh, mod, p, *, on_grid, batch):
    tokens = x2d.shape[0]
    seg, stride = _v_conv_geometry(on_grid)
    consts = [p["g_pre_mix"], p["g_post_mix"], p["dn_norm_g"], p["w_z"], p["w_bg"], p["w_mgate"],
              p["w_dn_out"], p["w_cf_out"], p["w_sc_out"], p["w_o"], p["sc_conv"]]
    tiled = [x2d, o_f, o_b, hc, cgxh]
    out = pl.pallas_call(
        functools.partial(_mixer_out_kernel, seg=seg, stride=stride),
        out_shape=jax.ShapeDtypeStruct(_tile_view(x2d, on_grid, batch).shape, F32),
        grid=(tokens // TOKENS_PER_TILE,),
        in_specs=[_tile_spec(a.shape[1], on_grid) for a in tiled] + [_mod_spec(mod, on_grid)]
                 + [_const_spec(c) for c in consts],
        out_specs=_tile_spec(D_MODEL, on_grid),
        scratch_shapes=[pltpu.VMEM((TOKENS_PER_TILE, D_MODEL), BF16)],
        compiler_params=pltpu.CompilerParams(dimension_semantics=("parallel",),
                                             vmem_limit_bytes=VMEM_LIMIT_BYTES),
        name="mixer_out",
    )(*[_tile_view(a, on_grid, batch) for a in tiled], mod, *map(_const_arg, consts))
    return out.reshape(tokens, D_MODEL)


def _ffn(x2d, mod, p, *, on_grid, batch):
    tokens = x2d.shape[0]
    seg, stride = _v_conv_geometry(on_grid)
    consts = [p["g_pre_ffn"], p["g_post_ffn"], p["w_ffn_up"], p["w_ffn_down"], p["ffn_conv"]]
    out = pl.pallas_call(
        functools.partial(_ffn_kernel, seg=seg, stride=stride),
        out_shape=jax.ShapeDtypeStruct(_tile_view(x2d, on_grid, batch).shape, F32),
        grid=(tokens // TOKENS_PER_TILE,),
        in_specs=[_tile_spec(D_MODEL, on_grid), _mod_spec(mod, on_grid)] + [_const_spec(c) for c in consts],
        out_specs=_tile_spec(D_MODEL, on_grid),
        scratch_shapes=[pltpu.VMEM((TOKENS_PER_TILE, D_FF), BF16)],
        compiler_params=pltpu.CompilerParams(dimension_semantics=("parallel",),
                                             vmem_limit_bytes=VMEM_LIMIT_BYTES),
        name="ffn",
    )(_tile_view(x2d, on_grid, batch), mod, *map(_const_arg, consts))
    return out.reshape(tokens, D_MODEL)


_W_IN_PARTS = (("w_qkv", _OFF_Q, _OFF_Z), ("w_z", _OFF_Z, _OFF_A), ("w_dngate", _OFF_A, _OFF_CF),
               ("w_cf", _OFF_CF, _OFF_SC), ("w_bg", _OFF_SC, _OFF_SC + SC_WIDTH),
               ("w_sc", _OFF_SC + SC_WIDTH, _OFF_GATE), ("w_mgate", _OFF_GATE, _OFF_GATE + 3 * D_MODEL))


def _split_w_in_kernel(wt_ref, *out_refs, parts):
    wt = wt_ref[0]
    for out_ref, (_, lo, hi) in zip(out_refs, parts):
        out_ref[0] = wt[lo:hi, :].T.astype(BF16)


def _split_w_in(w_in):
    depth, rows, cols = w_in.shape
    rb = SPLIT_ROW_BLOCK
    parts = [part for part in _W_IN_PARTS if (part[2] - part[1]) % LANES == 0]
    assert all(lo % SUBLANES == 0 for _, lo, _ in parts)
    widths = [hi - lo for _, lo, hi in parts]
    outs = pl.pallas_call(
        functools.partial(_split_w_in_kernel, parts=parts),
        out_shape=tuple(jax.ShapeDtypeStruct((depth, rows, width), BF16) for width in widths),
        grid=(depth, rows // rb),
        in_specs=[pl.BlockSpec((1, cols, rb), lambda l, r: (l, 0, r))],
        out_specs=tuple(pl.BlockSpec((1, rb, width), lambda l, r: (l, r, 0)) for width in widths),
        compiler_params=pltpu.CompilerParams(dimension_semantics=("parallel", "parallel"),
                                             vmem_limit_bytes=VMEM_LIMIT_BYTES),
        name="split_w_in",
    )(jnp.swapaxes(w_in, 1, 2))
    split = {name: out for (name, _, _), out in zip(parts, outs)}
    for name, lo, hi in _W_IN_PARTS:
        if name not in split:
            split[name] = w_in[:, :, lo:hi].astype(BF16)
    return split


def _matrix_weights(w_in, w_dn_out, w_cf_out, w_sc_out, w_o, w_ffn_up, w_ffn_down):
    stacked = _split_w_in(w_in)
    stacked.update(w_dn_out=w_dn_out.astype(BF16), w_cf_out=w_cf_out.astype(BF16), w_sc_out=w_sc_out.astype(BF16),
                   w_o=w_o.astype(BF16), w_ffn_up=w_ffn_up.astype(BF16), w_ffn_down=w_ffn_down.astype(BF16))
    return stacked


def _layer_params(l, matrices, g_pre_mix, g_post_mix, g_pre_ffn, g_post_ffn, dn_conv, dn_a_log, dn_dt_bias,
                  dn_norm_g, cf_conv, cf_ln_g, cf_ln_b, sc_conv, ffn_conv):
    row = lambda v: v.reshape(1, -1).astype(F32)
    return {
        **{name: (stacked, l) for name, stacked in matrices.items()},
        "g_pre_mix": row(g_pre_mix[l]), "g_post_mix": row(g_post_mix[l]),
        "g_pre_ffn": row(g_pre_ffn[l]), "g_post_ffn": row(g_post_ffn[l]),
        "dn_conv": dn_conv[l], "dn_a_log": row(dn_a_log[l]), "dn_dt_bias": row(dn_dt_bias[l]),
        "dn_norm_g": row(dn_norm_g[l]), "cf_conv": cf_conv[l], "cf_ln_g": row(cf_ln_g[l]),
        "cf_ln_b": row(cf_ln_b[l]), "sc_conv": sc_conv[l], "ffn_conv": ffn_conv[l],
    }


def _trunk_layer(x2d, mod, p, s0, *, batch, seq_len, on_grid):
    seg_h = GRID_W if on_grid else seq_len
    qkv, pack, hc, cgxh = _mixer_in(x2d, mod, p, seq_len=seq_len, seg=seg_h)
    o_f, o_b, states = _deltanet(qkv, pack, s0, batch=batch, seq_len=seq_len)
    x2d = _mixer_out(x2d, o_f, o_b, hc, cgxh, mod, p, on_grid=on_grid, batch=batch)
    x2d = _ffn(x2d, mod, p, on_grid=on_grid, batch=batch)
    return x2d, states


def kernel(x_prompt, x_sample, state_dn, c, c_ctx, w_mod, b_mod, g_pre_mix, g_post_mix, g_pre_ffn, g_post_ffn,
           w_in, dn_conv, dn_a_log, dn_dt_bias, dn_norm_g, w_dn_out, cf_conv, cf_ln_g, cf_ln_b, w_cf_out,
           sc_conv, w_sc_out, w_o, w_ffn_up, ffn_conv, w_ffn_down):
    batch, seq, _ = x_prompt.shape
    dec_batch, dec_seq, _ = x_sample.shape
    assert dec_batch + 1 <= SUBLANES
    c_all = jnp.concatenate([c, c_ctx[None, :], jnp.zeros((SUBLANES - dec_batch - 1, D_MODEL), F32)], axis=0)
    mods = _modulation(c_all, w_mod, b_mod)
    xp = x_prompt.reshape(batch * seq, D_MODEL)
    xs = x_sample.reshape(dec_batch * dec_seq, D_MODEL)
    ctx_states = []
    matrices = _matrix_weights(w_in, w_dn_out, w_cf_out, w_sc_out, w_o, w_ffn_up, w_ffn_down)
    for l in range(DEPTH):
        p = _layer_params(l, matrices, g_pre_mix, g_post_mix, g_pre_ffn, g_post_ffn, dn_conv, dn_a_log,
                          dn_dt_bias, dn_norm_g, cf_conv, cf_ln_g, cf_ln_b, sc_conv, ffn_conv)
        mod_lat = mods[l, :dec_batch].reshape(dec_batch, 6, D_MODEL)
        mod_ctx = mods[l, dec_batch:dec_batch + 1].reshape(1, 6, D_MODEL)
        xp, st = _trunk_layer(xp, mod_ctx, p, None, batch=batch, seq_len=seq, on_grid=False)
        ctx_states.append(st)
        xs, _ = _trunk_layer(xs, mod_lat, p, state_dn[:, l].astype(F32), batch=dec_batch, seq_len=dec_seq,
                             on_grid=True)
    new_state = jnp.stack(ctx_states, axis=1).astype(x_prompt.dtype)
    return (xp.reshape(batch, seq, D_MODEL), xs.reshape(dec_batch, dec_seq, D_MODEL), new_state)
```

```python
import functools

import numpy as np
import jax
import jax.numpy as jnp
from jax import lax
from jax.experimental import pallas as pl
from jax.experimental.pallas import tpu as pltpu

F32 = jnp.float32
BF16 = jnp.bfloat16

D_MODEL = 1024
DEPTH = 2
GRID_W = 64
DN_HEADS = 4
DN_DK = 128
DN_DV = 128
DN_BLOCK = 128
DN_STEP_TOKENS = 256
DN_BATCH_PER_STEP = 2
DN_QK = DN_HEADS * DN_DK
DN_V = DN_HEADS * DN_DV
CF_WIDTH = 512
CF_CONV_W = 31
SC_WIDTH = 512
D_FF = 2816
EPS = 1e-6

_OFF_Q = 0
_OFF_Z = 2 * DN_QK + DN_V
_OFF_A = _OFF_Z + DN_V
_OFF_CF = _OFF_A + 4 * DN_HEADS
_OFF_SC = _OFF_CF + 2 * CF_WIDTH
_OFF_GATE = _OFF_SC + 3 * SC_WIDTH

SUBLANES = 8
BF16_SUBLANES = 16
LANES = 128
TOKENS_PER_TILE = 512
COL_TILE_W = 8
CONV_GAP = 16
ROW_BLOCK = 64
FF_BLOCK = 256
MERGE_COL_BLOCK = 256
MOD_COL_BLOCK = 1024
SPLIT_ROW_BLOCK = 256
PACK_W = 128
VMEM_LIMIT_BYTES = 56 * 1024 * 1024


def _sigmoid(x):
    return 0.5 * jnp.tanh(0.5 * x) + 0.5


def _silu(x):
    h = 0.5 * x
    return h + h * jnp.tanh(h)


def _softplus(x):
    return jnp.maximum(x, 0.0) + jnp.log1p(jnp.exp(-jnp.abs(x)))


def _mm(a, b):
    return jnp.dot(a.astype(BF16), b.astype(BF16), preferred_element_type=F32)


def _mm_nt(a, b):
    return lax.dot_general(a.astype(BF16), b.astype(BF16), (((1,), (1,)), ((), ())),
                           preferred_element_type=F32)


def _modulated_rmsnorm(x, gain, shift, scale):
    y = x * lax.rsqrt(jnp.mean(x * x, axis=-1, keepdims=True) + EPS)
    return y * (gain * (1.0 + scale)) + shift


def _rmsnorm(x, gain):
    return x * lax.rsqrt(jnp.mean(x * x, axis=-1, keepdims=True) + EPS) * gain


def _store_segments(pad_ref, val, seg, gap):
    rows, width = val.shape
    zeros = jnp.zeros((gap, width), F32)
    pad_ref[pl.ds(0, gap), pl.ds(0, width)] = zeros
    for s in range(rows // seg):
        base = gap + s * (seg + gap)
        pad_ref[pl.ds(base, seg), pl.ds(0, width)] = val[s * seg:(s + 1) * seg]
        pad_ref[pl.ds(base + seg, gap), pl.ds(0, width)] = zeros


def _conv_block(pad_ref, w_ref, base, rows, c0, cols, w_c0):
    taps = w_ref.shape[0]
    offs = [k - taps // 2 for k in range(taps)]
    halo = SUBLANES * (-(-max(offs) // SUBLANES))
    n = rows + 2 * halo
    win = pad_ref[pl.ds(base - halo, n), pl.ds(c0, cols)]
    by_residue = {}
    for k, off in enumerate(offs):
        by_residue.setdefault(off % SUBLANES, []).append((k, off // SUBLANES))
    acc = None
    for b in sorted(by_residue):
        rolled = win if b == 0 else pltpu.roll(win, n - b, axis=0)
        for k, a in by_residue[b]:
            start = halo + SUBLANES * a
            term = rolled[start:start + rows] * w_ref[pl.ds(k, 1), pl.ds(w_c0, cols)]
            acc = term if acc is None else acc + term
    return acc


def _pad_rows(u):
    zeros = jnp.zeros((SUBLANES, u.shape[1]), u.dtype)
    return jnp.concatenate([zeros, u, zeros], axis=0)


def _conv3_block(u_pad, w, seg, stride, r0, rows):
    cur = u_pad[r0 + SUBLANES:r0 + SUBLANES + rows]
    if stride == SUBLANES:
        prev = u_pad[r0:r0 + rows]
        nxt = u_pad[r0 + 2 * SUBLANES:r0 + 2 * SUBLANES + rows]
    else:
        assert stride == 1 and seg % rows == 0
        n = rows + 2 * SUBLANES
        win = u_pad[r0:r0 + n]
        prev = pltpu.roll(win, 1, axis=0)[SUBLANES:SUBLANES + rows]
        nxt = pltpu.roll(win, n - 1, axis=0)[SUBLANES:SUBLANES + rows]
        sub = lax.broadcasted_iota(jnp.int32, (SUBLANES, u_pad.shape[1]), 0)
        if r0 % seg == 0:
            prev = jnp.concatenate([jnp.where(sub == 0, 0.0, prev[:SUBLANES]), prev[SUBLANES:]], axis=0)
        if (r0 + rows) % seg == 0:
            nxt = jnp.concatenate([nxt[:-SUBLANES], jnp.where(sub == SUBLANES - 1, 0.0, nxt[-SUBLANES:])], axis=0)
    return prev * w[0:1] + cur * w[1:2] + nxt * w[2:3]


def _mod_kernel(c_ref, w_ref, b_ref, o_ref):
    s = _silu(c_ref[...])
    o_ref[0] = _mm(s, w_ref[0]) + b_ref[0]


def _modulation(c_all, w_mod, b_mod):
    tn = MOD_COL_BLOCK
    n_cols = 6 * D_MODEL
    return pl.pallas_call(
        _mod_kernel,
        out_shape=jax.ShapeDtypeStruct((DEPTH, SUBLANES, n_cols), F32),
        grid=(DEPTH, n_cols // tn),
        in_specs=[
            pl.BlockSpec((SUBLANES, D_MODEL), lambda l, j: (0, 0)),
            pl.BlockSpec((1, D_MODEL, tn), lambda l, j: (l, 0, j)),
            pl.BlockSpec((1, 1, tn), lambda l, j: (l, 0, j)),
        ],
        out_specs=pl.BlockSpec((1, SUBLANES, tn), lambda l, j: (l, 0, j)),
        compiler_params=pltpu.CompilerParams(dimension_semantics=("parallel", "parallel"),
                                             vmem_limit_bytes=VMEM_LIMIT_BYTES),
        name="modulation",
    )(c_all, w_mod, b_mod.reshape(DEPTH, 1, n_cols))


def _mixer_in_kernel(x_ref, mod_ref, gpre_ref, wqkv_ref, wgate_ref, wcf_ref, wsc_ref,
                     dnconv_ref, cfconv_ref, lng_ref, lnb_ref, alog_ref, dtb_ref,
                     trif_ref, trib_ref, ones_ref,
                     qkv_out, pack_out, hc_out, cgxh_out,
                     pad_ref, *, seg):
    tm = x_ref.shape[0]
    nseg = tm // seg
    gap = CONV_GAP
    hb = _modulated_rmsnorm(x_ref[...], gpre_ref[...], mod_ref[0, 0:1, :], mod_ref[0, 1:2, :]).astype(BF16)

    for grp in range(3):
        proj = jnp.dot(hb, wqkv_ref[:, grp * DN_QK:(grp + 1) * DN_QK], preferred_element_type=F32)
        _store_segments(pad_ref, proj, seg, gap)
        for s in range(nseg):
            for r0 in range(0, seg, ROW_BLOCK):
                base = gap + s * (seg + gap) + r0
                rows = pl.ds(s * seg + r0, ROW_BLOCK)
                for hd in range(DN_HEADS):
                    col = grp * DN_QK + hd * DN_DK
                    a = _silu(_conv_block(pad_ref, dnconv_ref, base, ROW_BLOCK, hd * DN_DK, DN_DK, col))
                    if grp < 2:
                        inv_norm = lax.rsqrt(jnp.sum(a * a, axis=-1, keepdims=True) + EPS)
                        a = a * (inv_norm * DN_DK ** -0.5 if grp == 0 else inv_norm)
                    qkv_out[rows, pl.ds(col, DN_DK)] = a

    nh2 = 2 * DN_HEADS
    ab = jnp.dot(hb, wgate_ref[...], preferred_element_type=F32)
    g = -jnp.exp(alog_ref[...]) * _softplus(ab[:, 0:nh2] + dtb_ref[...])
    beta = _sigmoid(ab[:, nh2:2 * nh2])
    g1 = g.astype(BF16)
    r1 = g - g1.astype(F32)
    g2 = r1.astype(BF16)
    g3 = (r1 - g2.astype(F32)).astype(BF16)
    pieces = jnp.concatenate([g1, g2, g3], axis=1)

    def summed(mat_ref):
        c = jnp.dot(mat_ref[...], pieces, preferred_element_type=F32)
        return c[:, 0:nh2] + c[:, nh2:2 * nh2] + c[:, 2 * nh2:3 * nh2]

    lane = lax.broadcasted_iota(jnp.int32, (tm, nh2), 1)
    gam = jnp.where(lane < DN_HEADS, summed(trif_ref), summed(trib_ref))
    tot = summed(ones_ref)
    expg = jnp.exp(gam)
    pack_out[...] = jnp.concatenate(
        [gam, beta, expg, beta * expg, jnp.exp(tot - gam), jnp.exp(tot),
         jnp.zeros((tm, PACK_W - 6 * nh2), F32)], axis=1)

    pc = jnp.dot(hb, wcf_ref[...], preferred_element_type=F32)
    glu = pc[:, :CF_WIDTH] * _sigmoid(pc[:, CF_WIDTH:])
    _store_segments(pad_ref, glu, seg, gap)

    ps = jnp.dot(hb, wsc_ref[...], preferred_element_type=F32)
    cgxh_out[...] = ps[:, :SC_WIDTH] * ps[:, SC_WIDTH:]

    for s in range(nseg):
        for r0 in range(0, seg, ROW_BLOCK):
            base = gap + s * (seg + gap) + r0
            c = jnp.concatenate([_conv_block(pad_ref, cfconv_ref, base, ROW_BLOCK, c0, LANES, c0)
                                 for c0 in range(0, CF_WIDTH, LANES)], axis=1)
            mu = jnp.mean(c, axis=-1, keepdims=True)
            cc = c - mu
            y = cc * lax.rsqrt(jnp.mean(cc * cc, axis=-1, keepdims=True) + EPS)
            y = y * lng_ref[...] + lnb_ref[...]
            hc_out[pl.ds(s * seg + r0, ROW_BLOCK), :] = _silu(y)


def _chunk_matrices(tm):
    idx = np.arange(tm)
    same = (idx[:, None] // DN_BLOCK) == (idx[None, :] // DN_BLOCK)
    lower = same & (idx[:, None] >= idx[None, :])
    upper = same & (idx[:, None] <= idx[None, :])
    as_bf16 = lambda m: jnp.asarray(m.astype(np.float32), dtype=BF16)
    return as_bf16(lower), as_bf16(upper), as_bf16(same)


def _const_spec(operand):
    if isinstance(operand, tuple):
        stacked, layer = operand
        rest = stacked.ndim - 1
        return pl.BlockSpec((None,) + stacked.shape[1:], lambda *_: (layer,) + (0,) * rest,
                            pipeline_mode=pl.Buffered(1))
    nd = operand.ndim
    return pl.BlockSpec(operand.shape, lambda *_: (0,) * nd, pipeline_mode=pl.Buffered(1))


def _const_arg(operand):
    return operand[0] if isinstance(operand, tuple) else operand


def _mixer_in(x2d, mod, p, *, seq_len, seg):
    tokens = x2d.shape[0]
    tm = TOKENS_PER_TILE
    tiles_per_mod = max(seq_len // tm, 1) if mod.shape[0] > 1 else None
    mod_map = (lambda i: (i // tiles_per_mod, 0, 0)) if tiles_per_mod else (lambda i: (0, 0, 0))
    trif, trib, ones = _chunk_matrices(tm)
    pad_rows = CONV_GAP + (tm // seg) * (seg + CONV_GAP)
    row = lambda i: (i, 0)
    consts = [p["g_pre_mix"], p["w_qkv"], p["w_dngate"], p["w_cf"], p["w_sc"], p["dn_conv"], p["cf_conv"],
              p["cf_ln_g"], p["cf_ln_b"], p["dn_a_log"], p["dn_dt_bias"], trif, trib, ones]
    return pl.pallas_call(
        functools.partial(_mixer_in_kernel, seg=seg),
        out_shape=(jax.ShapeDtypeStruct((tokens, 2 * DN_QK + DN_V), F32),
                   jax.ShapeDtypeStruct((tokens, PACK_W), F32),
                   jax.ShapeDtypeStruct((tokens, CF_WIDTH), F32),
                   jax.ShapeDtypeStruct((tokens, SC_WIDTH), F32)),
        grid=(tokens // tm,),
        in_specs=[pl.BlockSpec((tm, D_MODEL), row), pl.BlockSpec((1, 6, D_MODEL), mod_map)]
                 + [_const_spec(c) for c in consts],
        out_specs=(pl.BlockSpec((tm, 2 * DN_QK + DN_V), row), pl.BlockSpec((tm, PACK_W), row),
                   pl.BlockSpec((tm, CF_WIDTH), row), pl.BlockSpec((tm, SC_WIDTH), row)),
        scratch_shapes=[pltpu.VMEM((pad_rows, DN_QK), F32)],
        compiler_params=pltpu.CompilerParams(dimension_semantics=("parallel",),
                                             vmem_limit_bytes=VMEM_LIMIT_BYTES),
        name="mixer_in",
    )(x2d, mod, *map(_const_arg, consts))


def _substitution_masks():
    n = DN_BLOCK
    i = np.arange(n)[:, None]
    j = np.arange(n)[None, :]
    out = []
    for strict in (i > j, i < j):
        levels = [strict & (i // 2 == j // 2)]
        size = 2
        while size < n:
            levels.append(strict & (i // (2 * size) == j // (2 * size)) & (i // size != j // size))
            size *= 2
        out.append(np.stack(levels))
    return jnp.asarray(np.stack(out).astype(np.float32), dtype=BF16)


def _deltanet_kernel(qf_ref, qb_ref, pf_ref, pb_ref, masks_ref, *rest, has_s0, nbatch, same_tokens):
    if has_s0:
        s0_ref, of_ref, ob_ref, sout_ref, state_ref = rest
    else:
        of_ref, ob_ref, sout_ref, state_ref = rest
    n = pl.program_id(1)
    last = pl.num_programs(1) - 1
    c = DN_BLOCK
    nh2 = 2 * DN_HEADS
    n_levels = masks_ref.shape[1]

    @pl.when(n == 0)
    def _():
        if has_s0:
            state_ref[...] = s0_ref[...]
        else:
            state_ref[...] = jnp.zeros(state_ref.shape, F32)

    ii = lax.broadcasted_iota(jnp.int32, (c, c), 0)
    jj = lax.broadcasted_iota(jnp.int32, (c, c), 1)
    eye = (ii == jj).astype(F32).astype(BF16)
    incl = ((ii >= jj), (ii <= jj))
    strict = ((ii > jj), (ii < jj))
    q_refs, p_refs, o_refs = (qf_ref, qb_ref), (pf_ref, pb_ref), (of_ref, ob_ref)

    nblk = qf_ref.shape[1] // c
    chains = [(bi, d, hd) for bi in range(nbatch) for d in (0, 1) for hd in range(DN_HEADS)]
    probs = [(bi, d, blk, hd) for bi, d, hd in chains for blk in range(nblk)]
    rows = lambda blk: pl.ds(blk * c, c)
    pk = {(bi, d, blk): p_refs[d][bi, rows(blk), :]
          for bi in range(nbatch) for d in (0, 1) for blk in range(nblk)}
    pk_t = {key: val.T for key, val in pk.items()}

    def col(p, grp):
        bi, d, blk, hd = p
        j = grp * nh2 + d * DN_HEADS + hd
        return pk[(bi, d, blk)][:, j:j + 1]

    def head(p, off):
        bi, d, blk, hd = p
        return q_refs[d][bi, rows(blk), pl.ds(off + hd * DN_DK, DN_DK)]

    q = {p: head(p, 0) for p in probs}
    k = {p: head(p, DN_QK) for p in probs}
    v = {p: head(p, 2 * DN_QK) for p in probs}
    kb = {p: k[p].astype(BF16) for p in probs}
    kq = {}
    for p in probs:
        bi, d, blk, hd = p
        twin = (bi, 0, blk, hd)
        if same_tokens and d == 1:
            kq[p] = kq[twin]
        else:
            kq[p] = _mm_nt(jnp.concatenate([kb[p], q[p].astype(BF16)], axis=0), kb[p])
    kk = {p: kq[p][:c] for p in probs}
    e = {}
    for p in probs:
        bi, d, blk, hd = p
        gam_r = pk_t[(bi, d, blk)][d * DN_HEADS + hd:d * DN_HEADS + hd + 1, :]
        e[p] = jnp.exp(jnp.where(incl[d], col(p, 0) - gam_r, 0.0))
    a = {p: jnp.where(strict[p[1]], col(p, 1) * kk[p] * e[p], 0.0).astype(BF16) for p in probs}
    inv = {p: eye - a[p] * masks_ref[p[1], 0] for p in probs}
    for lvl in range(1, n_levels):
        size = 2 ** lvl
        off = {p: a[p] * masks_ref[p[1], lvl] for p in probs}
        if size % BF16_SUBLANES:
            t = {p: jnp.dot(inv[p], off[p], preferred_element_type=F32).astype(BF16) for p in probs}
            inv = {p: inv[p] - jnp.dot(t[p], inv[p], preferred_element_type=F32).astype(BF16) for p in probs}
            continue
        halves = {p: inv[p].reshape(c // (2 * size), 2, size, c) for p in probs}
        moving = {p: halves[p][:, 1 - p[1]].reshape(c // 2, c) for p in probs}
        t = {p: jnp.dot(moving[p], off[p], preferred_element_type=F32).astype(BF16) for p in probs}
        moved = {p: (moving[p] - jnp.dot(t[p], inv[p], preferred_element_type=F32).astype(BF16)
                     ).reshape(c // (2 * size), 1, size, c) for p in probs}
        inv = {p: jnp.concatenate([halves[p][:, 0:1], moved[p]] if p[1] == 0 else [moved[p], halves[p][:, 1:2]],
                                  axis=1).reshape(c, c) for p in probs}
    rhs = {p: jnp.concatenate([col(p, 1) * v[p], col(p, 3) * k[p]], axis=1).astype(BF16) for p in probs}
    sol = {p: jnp.dot(inv[p], rhs[p], preferred_element_type=F32) for p in probs}
    qkd = {p: (kq[p][c:] * jnp.where(incl[p[1]], e[p], 0.0)).astype(BF16) for p in probs}
    with_state = {p: jnp.concatenate([sol[p][:, DN_DV:].astype(BF16), (q[p] * col(p, 2)).astype(BF16)], axis=0)
                  for p in probs}
    with_v_new = {p: jnp.concatenate([qkd[p], (k[p] * col(p, 4)).T.astype(BF16)], axis=0) for p in probs}

    state = {ch: state_ref[ch] for ch in chains}
    for step in range(nblk):
        cur = {(bi, d, hd): (bi, d, step if d == 0 else nblk - 1 - step, hd) for bi, d, hd in chains}
        from_state = {ch: jnp.dot(with_state[cur[ch]], state[ch].astype(BF16), preferred_element_type=F32)
                      for ch in chains}
        v_new = {ch: (sol[cur[ch]][:, :DN_DV] - from_state[ch][:c]).astype(BF16) for ch in chains}
        from_v_new = {ch: jnp.dot(with_v_new[cur[ch]], v_new[ch], preferred_element_type=F32)
                      for ch in chains}
        new_state = {}
        for ch in chains:
            bi, d, blk, hd = cur[ch]
            o_refs[d][bi, rows(blk), pl.ds(hd * DN_DV, DN_DV)] = from_state[ch][c:] + from_v_new[ch][:c]
            j = 5 * nh2 + d * DN_HEADS + hd
            etot = pk[(bi, d, blk)][0:1, j:j + 1]
            new_state[ch] = state[ch] * etot + from_v_new[ch][c:]
        state = new_state
    for ch in chains:
        state_ref[ch] = state[ch]

    @pl.when(n == last)
    def _():
        sout_ref[...] = state_ref[...]


def _deltanet(qkv, pack, s0, *, batch, seq_len):
    tokens = qkv.shape[0]
    step = min(DN_STEP_TOKENS, seq_len)
    nb = seq_len // step
    nbatch = DN_BATCH_PER_STEP if batch % DN_BATCH_PER_STEP == 0 else 1
    width = 2 * DN_QK + DN_V
    fwd = lambda b, n: (b, n, 0)
    bwd = lambda b, n: (b, nb - 1 - n, 0)
    state_block = (nbatch, 2, DN_HEADS, DN_DK, DN_DV)
    state_map = lambda b, n: (b, 0, 0, 0, 0)
    masks = _substitution_masks()
    qkv3 = qkv.reshape(batch, seq_len, width)
    pack3 = pack.reshape(batch, seq_len, PACK_W)
    in_specs = [pl.BlockSpec((nbatch, step, width), fwd), pl.BlockSpec((nbatch, step, width), bwd),
                pl.BlockSpec((nbatch, step, PACK_W), fwd), pl.BlockSpec((nbatch, step, PACK_W), bwd),
                _const_spec(masks)]
    args = [qkv3, qkv3, pack3, pack3, masks]
    if s0 is not None:
        in_specs.append(pl.BlockSpec(state_block, state_map))
        args.append(s0)
    o_f, o_b, states = pl.pallas_call(
        functools.partial(_deltanet_kernel, has_s0=s0 is not None, nbatch=nbatch, same_tokens=nb == 1),
        out_shape=(jax.ShapeDtypeStruct((batch, seq_len, DN_V), F32),
                   jax.ShapeDtypeStruct((batch, seq_len, DN_V), F32),
                   jax.ShapeDtypeStruct((batch, 2, DN_HEADS, DN_DK, DN_DV), F32)),
        grid=(batch // nbatch, nb),
        in_specs=in_specs,
        out_specs=(pl.BlockSpec((nbatch, step, DN_V), fwd), pl.BlockSpec((nbatch, step, DN_V), bwd),
                   pl.BlockSpec(state_block, state_map)),
        scratch_shapes=[pltpu.VMEM(state_block, F32)],
        compiler_params=pltpu.CompilerParams(dimension_semantics=("parallel", "arbitrary"),
                                             vmem_limit_bytes=VMEM_LIMIT_BYTES),
        name="deltanet",
    )(*args)
    return o_f.reshape(tokens, DN_V), o_b.reshape(tokens, DN_V), states


def _mixer_out_kernel(x_ref, of_ref, ob_ref, hc_ref, cg_ref, mod_ref, gpre_ref, gpost_ref, dnn_ref,
                      wz_ref, wbg_ref, wmg_ref, wdn_ref, wcf_ref, wsc_ref, wo_ref, scconv_ref,
                      out_ref, min_ref, *, seg, stride):
    tm = TOKENS_PER_TILE
    x = x_ref[...].reshape(tm, D_MODEL)
    hb = _modulated_rmsnorm(x, gpre_ref[...], mod_ref[0, 0:1, :], mod_ref[0, 1:2, :]).astype(BF16)

    o = of_ref[...].reshape(tm, DN_V) + ob_ref[...].reshape(tm, DN_V)
    z = jnp.dot(hb, wz_ref[...], preferred_element_type=F32)
    heads = []
    for hd in range(DN_HEADS):
        oh = _rmsnorm(o[:, hd * DN_DV:(hd + 1) * DN_DV], dnn_ref[...])
        heads.append((oh * _silu(z[:, hd * DN_DV:(hd + 1) * DN_DV])).astype(BF16))
    o_gated = jnp.concatenate(heads, axis=1)

    cg_pad = _pad_rows(cg_ref[...].reshape(tm, SC_WIDTH))
    bg = jnp.dot(hb, wbg_ref[...], preferred_element_type=F32)
    w_sc = scconv_ref[...]
    sc_parts = [_conv3_block(cg_pad, w_sc, seg, stride, r0, ROW_BLOCK) * bg[r0:r0 + ROW_BLOCK]
                for r0 in range(0, tm, ROW_BLOCK)]
    sc_in = jnp.concatenate(sc_parts, axis=0).astype(BF16)
    hc = hc_ref[...].reshape(tm, CF_WIDTH).astype(BF16)

    cb = MERGE_COL_BLOCK
    for c0 in range(0, D_MODEL, cb):
        cols = slice(c0, c0 + cb)
        y_dn = jnp.dot(o_gated, wdn_ref[:, cols], preferred_element_type=F32)
        y_cf = jnp.dot(hc, wcf_ref[:, cols], preferred_element_type=F32)
        y_sc = jnp.dot(sc_in, wsc_ref[:, cols], preferred_element_type=F32)
        ga = _sigmoid(jnp.dot(hb, wmg_ref[:, c0:c0 + cb], preferred_element_type=F32))
        gb = _sigmoid(jnp.dot(hb, wmg_ref[:, D_MODEL + c0:D_MODEL + c0 + cb], preferred_element_type=F32))
        gc = _sigmoid(jnp.dot(hb, wmg_ref[:, 2 * D_MODEL + c0:2 * D_MODEL + c0 + cb], preferred_element_type=F32))
        min_ref[:, cols] = (ga * y_dn + gb * y_cf + gc * y_sc).astype(BF16)

    m = jnp.dot(min_ref[...], wo_ref[...], preferred_element_type=F32)
    out = x + mod_ref[0, 2:3, :] * _rmsnorm(m, gpost_ref[...])
    out_ref[...] = out.reshape(out_ref.shape)


def _ffn_kernel(x_ref, mod_ref, gpre_ref, gpost_ref, wup_ref, wdown_ref, conv_ref,
                out_ref, act_ref, *, seg, stride):
    tm = TOKENS_PER_TILE
    x = x_ref[...].reshape(tm, D_MODEL)
    hb = _modulated_rmsnorm(x, gpre_ref[...], mod_ref[0, 3:4, :], mod_ref[0, 4:5, :]).astype(BF16)
    for f0 in range(0, D_FF, FF_BLOCK):
        ua = _pad_rows(jnp.dot(hb, wup_ref[:, f0:f0 + FF_BLOCK], preferred_element_type=F32))
        ub = _pad_rows(jnp.dot(hb, wup_ref[:, D_FF + f0:D_FF + f0 + FF_BLOCK], preferred_element_type=F32))
        wa = conv_ref[:, f0:f0 + FF_BLOCK]
        wb = conv_ref[:, D_FF + f0:D_FF + f0 + FF_BLOCK]
        for r0 in range(0, tm, ROW_BLOCK):
            ca = _conv3_block(ua, wa, seg, stride, r0, ROW_BLOCK)
            cb = _conv3_block(ub, wb, seg, stride, r0, ROW_BLOCK)
            act_ref[pl.ds(r0, ROW_BLOCK), pl.ds(f0, FF_BLOCK)] = (_silu(ca) * cb).astype(BF16)
    y = jnp.dot(act_ref[...], wdown_ref[...], preferred_element_type=F32)
    out = x + mod_ref[0, 5:6, :] * _rmsnorm(y, gpost_ref[...])
    out_ref[...] = out.reshape(out_ref.shape)


def _mixer_out_ffn_kernel(*refs, n_mixer_out_in, seg, stride):
    mixer_out_in = refs[:n_mixer_out_in]
    ffn_consts = refs[n_mixer_out_in:-4]
    out_ref, min_ref, act_ref, mid_ref = refs[-4:]
    _mixer_out_kernel(*mixer_out_in, mid_ref, min_ref, seg=seg, stride=stride)
    _ffn_kernel(mid_ref, mixer_out_in[5], *ffn_consts, out_ref, act_ref, seg=seg, stride=stride)


def _tile_view(a2d, on_grid, batch):
    width = a2d.shape[1]
    if on_grid:
        return a2d.reshape(batch, a2d.shape[0] // (batch * GRID_W), GRID_W, width)
    return a2d.reshape(a2d.shape[0] // TOKENS_PER_TILE, TOKENS_PER_TILE // COL_TILE_W, COL_TILE_W, width)


def _tile_spec(width, on_grid):
    block = (1, TOKENS_PER_TILE // COL_TILE_W, COL_TILE_W, width)
    if on_grid:
        per_batch = GRID_W // COL_TILE_W
        return pl.BlockSpec(block, lambda i: (i // per_batch, 0, i % per_batch, 0))
    return pl.BlockSpec(block, lambda i: (i, 0, 0, 0))


def _mod_spec(mod, on_grid):
    if on_grid:
        per_batch = GRID_W // COL_TILE_W
        return pl.BlockSpec((1, 6, D_MODEL), lambda i: (i // per_batch, 0, 0))
    return pl.BlockSpec((1, 6, D_MODEL), lambda i: (0, 0, 0))


def _v_conv_geometry(on_grid):
    return (TOKENS_PER_TILE, COL_TILE_W) if on_grid else (256, 1)


def _mixer_out(x2d, o_f, o_b, hc, cgxh, mod, p, *, on_grid, batch):
    tokens = x2d.shape[0]
    seg, stride = _v_conv_geometry(on_grid)
    consts = [p["g_pre_mix"], p["g_post_mix"], p["dn_norm_g"], p["w_z"], p["w_bg"], p["w_mgate"],
              p["w_dn_out"], p["w_cf_out"], p["w_sc_out"], p["w_o"], p["sc_conv"]]
    tiled = [x2d, o_f, o_b, hc, cgxh]
    out = pl.pallas_call(
        functools.partial(_mixer_out_kernel, seg=seg, stride=stride),
        out_shape=jax.ShapeDtypeStruct(_tile_view(x2d, on_grid, batch).shape, F32),
        grid=(tokens // TOKENS_PER_TILE,),
        in_specs=[_tile_spec(a.shape[1], on_grid) for a in tiled] + [_mod_spec(mod, on_grid)]
                 + [_const_spec(c) for c in consts],
        out_specs=_tile_spec(D_MODEL, on_grid),
        scratch_shapes=[pltpu.VMEM((TOKENS_PER_TILE, D_MODEL), BF16)],
        compiler_params=pltpu.CompilerParams(dimension_semantics=("parallel",),
                                             vmem_limit_bytes=VMEM_LIMIT_BYTES),
        name="mixer_out",
    )(*[_tile_view(a, on_grid, batch) for a in tiled], mod, *map(_const_arg, consts))
    return out.reshape(tokens, D_MODEL)


def _ffn(x2d, mod, p, *, on_grid, batch):
    tokens = x2d.shape[0]
    seg, stride = _v_conv_geometry(on_grid)
    consts = [p["g_pre_ffn"], p["g_post_ffn"], p["w_ffn_up"], p["w_ffn_down"], p["ffn_conv"]]
    out = pl.pallas_call(
        functools.partial(_ffn_kernel, seg=seg, stride=stride),
        out_shape=jax.ShapeDtypeStruct(_tile_view(x2d, on_grid, batch).shape, F32),
        grid=(tokens // TOKENS_PER_TILE,),
        in_specs=[_tile_spec(D_MODEL, on_grid), _mod_spec(mod, on_grid)] + [_const_spec(c) for c in consts],
        out_specs=_tile_spec(D_MODEL, on_grid),
        scratch_shapes=[pltpu.VMEM((TOKENS_PER_TILE, D_FF), BF16)],
        compiler_params=pltpu.CompilerParams(dimension_semantics=("parallel",),
                                             vmem_limit_bytes=VMEM_LIMIT_BYTES),
        name="ffn",
    )(_tile_view(x2d, on_grid, batch), mod, *map(_const_arg, consts))
    return out.reshape(tokens, D_MODEL)


def _mixer_out_ffn(x2d, o_f, o_b, hc, cgxh, mod, p, *, on_grid, batch):
    tokens = x2d.shape[0]
    seg, stride = _v_conv_geometry(on_grid)
    mixer_consts = [p["g_pre_mix"], p["g_post_mix"], p["dn_norm_g"], p["w_z"], p["w_bg"], p["w_mgate"],
                    p["w_dn_out"], p["w_cf_out"], p["w_sc_out"], p["w_o"], p["sc_conv"]]
    ffn_consts = [p["g_pre_ffn"], p["g_post_ffn"], p["w_ffn_up"], p["w_ffn_down"], p["ffn_conv"]]
    tiled = [x2d, o_f, o_b, hc, cgxh]
    tile_shape = (1, TOKENS_PER_TILE // COL_TILE_W, COL_TILE_W, D_MODEL)
    out = pl.pallas_call(
        functools.partial(_mixer_out_ffn_kernel, n_mixer_out_in=len(tiled) + 1 + len(mixer_consts),
                          seg=seg, stride=stride),
        out_shape=jax.ShapeDtypeStruct(_tile_view(x2d, on_grid, batch).shape, F32),
        grid=(tokens // TOKENS_PER_TILE,),
        in_specs=[_tile_spec(a.shape[1], on_grid) for a in tiled] + [_mod_spec(mod, on_grid)]
                 + [_const_spec(c) for c in mixer_consts + ffn_consts],
        out_specs=_tile_spec(D_MODEL, on_grid),
        scratch_shapes=[pltpu.VMEM((TOKENS_PER_TILE, D_MODEL), BF16), pltpu.VMEM((TOKENS_PER_TILE, D_FF), BF16),
                        pltpu.VMEM(tile_shape, F32)],
        compiler_params=pltpu.CompilerParams(dimension_semantics=("parallel",),
                                             vmem_limit_bytes=VMEM_LIMIT_BYTES),
        name="mixer_out_ffn",
    )(*[_tile_view(a, on_grid, batch) for a in tiled], mod, *map(_const_arg, mixer_consts + ffn_consts))
    return out.reshape(tokens, D_MODEL)


_W_IN_PARTS = (("w_qkv", _OFF_Q, _OFF_Z), ("w_z", _OFF_Z, _OFF_A), ("w_dngate", _OFF_A, _OFF_CF),
               ("w_cf", _OFF_CF, _OFF_SC), ("w_bg", _OFF_SC, _OFF_SC + SC_WIDTH),
               ("w_sc", _OFF_SC + SC_WIDTH, _OFF_GATE), ("w_mgate", _OFF_GATE, _OFF_GATE + 3 * D_MODEL))


def _split_w_in_kernel(wt_ref, *out_refs, parts):
    wt = wt_ref[0]
    for out_ref, (_, lo, hi) in zip(out_refs, parts):
        out_ref[0] = wt[lo:hi, :].T.astype(BF16)


def _split_w_in(w_in):
    depth, rows, cols = w_in.shape
    rb = SPLIT_ROW_BLOCK
    parts = [part for part in _W_IN_PARTS if (part[2] - part[1]) % LANES == 0]
    assert all(lo % SUBLANES == 0 for _, lo, _ in parts)
    widths = [hi - lo for _, lo, hi in parts]
    outs = pl.pallas_call(
        functools.partial(_split_w_in_kernel, parts=parts),
        out_shape=tuple(jax.ShapeDtypeStruct((depth, rows, width), BF16) for width in widths),
        grid=(depth, rows // rb),
        in_specs=[pl.BlockSpec((1, cols, rb), lambda l, r: (l, 0, r))],
        out_specs=tuple(pl.BlockSpec((1, rb, width), lambda l, r: (l, r, 0)) for width in widths),
        compiler_params=pltpu.CompilerParams(dimension_semantics=("parallel", "parallel"),
                                             vmem_limit_bytes=VMEM_LIMIT_BYTES),
        name="split_w_in",
    )(jnp.swapaxes(w_in, 1, 2))
    split = {name: out for (name, _, _), out in zip(parts, outs)}
    for name, lo, hi in _W_IN_PARTS:
        if name not in split:
            split[name] = w_in[:, :, lo:hi].astype(BF16)
    return split


def _matrix_weights(w_in, w_dn_out, w_cf_out, w_sc_out, w_o, w_ffn_up, w_ffn_down):
    stacked = _split_w_in(w_in)
    stacked.update(w_dn_out=w_dn_out.astype(BF16), w_cf_out=w_cf_out.astype(BF16), w_sc_out=w_sc_out.astype(BF16),
                   w_o=w_o.astype(BF16), w_ffn_up=w_ffn_up.astype(BF16), w_ffn_down=w_ffn_down.astype(BF16))
    return stacked


def _layer_params(l, matrices, g_pre_mix, g_post_mix, g_pre_ffn, g_post_ffn, dn_conv, dn_a_log, dn_dt_bias,
                  dn_norm_g, cf_conv, cf_ln_g, cf_ln_b, sc_conv, ffn_conv):
    row = lambda v: v.reshape(1, -1).astype(F32)
    return {
        **{name: (stacked, l) for name, stacked in matrices.items()},
        "g_pre_mix": row(g_pre_mix[l]), "g_post_mix": row(g_post_mix[l]),
        "g_pre_ffn": row(g_pre_ffn[l]), "g_post_ffn": row(g_post_ffn[l]),
        "dn_conv": dn_conv[l], "dn_a_log": row(dn_a_log[l]), "dn_dt_bias": row(dn_dt_bias[l]),
        "dn_norm_g": row(dn_norm_g[l]), "cf_conv": cf_conv[l], "cf_ln_g": row(cf_ln_g[l]),
        "cf_ln_b": row(cf_ln_b[l]), "sc_conv": sc_conv[l], "ffn_conv": ffn_conv[l],
    }


def _trunk_layer(x2d, mod, p, s0, *, batch, seq_len, on_grid):
    seg_h = GRID_W if on_grid else seq_len
    qkv, pack, hc, cgxh = _mixer_in(x2d, mod, p, seq_len=seq_len, seg=seg_h)
    o_f, o_b, states = _deltanet(qkv, pack, s0, batch=batch, seq_len=seq_len)
    x2d = _mixer_out_ffn(x2d, o_f, o_b, hc, cgxh, mod, p, on_grid=on_grid, batch=batch)
    return x2d, states


def kernel(x_prompt, x_sample, state_dn, c, c_ctx, w_mod, b_mod, g_pre_mix, g_post_mix, g_pre_ffn, g_post_ffn,
           w_in, dn_conv, dn_a_log, dn_dt_bias, dn_norm_g, w_dn_out, cf_conv, cf_ln_g, cf_ln_b, w_cf_out,
           sc_conv, w_sc_out, w_o, w_ffn_up, ffn_conv, w_ffn_down):
    batch, seq, _ = x_prompt.shape
    dec_batch, dec_seq, _ = x_sample.shape
    assert dec_batch + 1 <= SUBLANES
    c_all = jnp.concatenate([c, c_ctx[None, :], jnp.zeros((SUBLANES - dec_batch - 1, D_MODEL), F32)], axis=0)
    mods = _modulation(c_all, w_mod, b_mod)
    xp = x_prompt.reshape(batch * seq, D_MODEL)
    xs = x_sample.reshape(dec_batch * dec_seq, D_MODEL)
    ctx_states = []
    matrices = _matrix_weights(w_in, w_dn_out, w_cf_out, w_sc_out, w_o, w_ffn_up, w_ffn_down)
    for l in range(DEPTH):
        p = _layer_params(l, matrices, g_pre_mix, g_post_mix, g_pre_ffn, g_post_ffn, dn_conv, dn_a_log,
                          dn_dt_bias, dn_norm_g, cf_conv, cf_ln_g, cf_ln_b, sc_conv, ffn_conv)
        mod_lat = mods[l, :dec_batch].reshape(dec_batch, 6, D_MODEL)
        mod_ctx = mods[l, dec_batch:dec_batch + 1].reshape(1, 6, D_MODEL)
        xp, st = _trunk_layer(xp, mod_ctx, p, None, batch=batch, seq_len=seq, on_grid=False)
        ctx_states.append(st)
        xs, _ = _trunk_layer(xs, mod_lat, p, state_dn[:, l].astype(F32), batch=dec_batch, seq_len=dec_seq,
                             on_grid=True)
    new_state = jnp.stack(ctx_states, axis=1).astype(x_prompt.dtype)
    return (xp.reshape(batch, seq, D_MODEL), xs.reshape(dec_batch, dec_seq, D_MODEL), new_state)
```
